```python
import jax, jax.numpy as jnp
from jax import lax
import numpy as np

D_MODEL = 1024
BATCH = 4
SEQ = 8192
DEPTH = 4

GRID_W = 64
CTX_LEN = 256
HEAD_DIM = 64
N_Q_HEADS = 8
N_KV_HEADS = 2
Q_PER_KV = N_Q_HEADS // N_KV_HEADS
ATT_WIDTH = N_Q_HEADS * HEAD_DIM
KV_WIDTH = N_KV_HEADS * HEAD_DIM
WINDOW = 128
BLOCK = 128
POOL_WINDOWS = (2, 4, 8, 16)
N_POOL_GROUPS = 4
POOL_WIDTH = D_MODEL // 2
POOL_GROUP = POOL_WIDTH // N_POOL_GROUPS
N_BRANCHES = 2
IN_WIDTH = ATT_WIDTH + 2 * KV_WIDTH + POOL_WIDTH + N_BRANCHES * D_MODEL
N_MOD = 6
D_FF = 2816
N_EXPERTS = 8
TOP_K = 2
D_FF_EXPERT = 3584
ROPE_BASE = 10000.0
EPS = 1e-6
N_DENSE = (DEPTH + 1) // 2
N_MOE = DEPTH // 2
NEG_INF = -1e30

kernel_name = "hybrid_gated_swa_pool_moe_dit"


def rms_norm(x, g):
    xf = x.astype(jnp.float32)
    y = xf * lax.rsqrt(jnp.mean(xf * xf, axis=-1, keepdims=True) + EPS)
    return (y * g.astype(jnp.float32)).astype(x.dtype)


def modulate(h, shift, scale):
    return h * (1 + scale) + shift


def rope_axis(x, pos):
    half = x.shape[-1] // 2
    freqs = ROPE_BASE ** (-jnp.arange(half, dtype=jnp.float32) / half)
    ang = pos.astype(jnp.float32)[:, None] * freqs[None, :]
    cos = jnp.cos(ang)[None, :, None, :].astype(x.dtype)
    sin = jnp.sin(ang)[None, :, None, :].astype(x.dtype)
    x1, x2 = x[..., :half], x[..., half:]
    return jnp.concatenate([x1 * cos - x2 * sin, x2 * cos + x1 * sin], axis=-1)


def rope_2d(x, row, col):
    r = HEAD_DIM // 2
    return jnp.concatenate([rope_axis(x[..., :r], row), rope_axis(x[..., r:], col)], axis=-1)


def sink_softmax(scores, sinks):
    sink = sinks.astype(jnp.float32).reshape(1, N_KV_HEADS, Q_PER_KV, 1, 1)
    lead = scores.ndim - 4
    sink = sink.reshape((1,) * (lead - 1) + sink.shape)
    sink = jnp.broadcast_to(sink, scores.shape[:-1] + (1,))
    p = jax.nn.softmax(jnp.concatenate([scores, sink], axis=-1), axis=-1)
    return p[..., :-1]


def split_proj(u):
    o1 = ATT_WIDTH
    o2 = o1 + KV_WIDTH
    o3 = o2 + KV_WIDTH
    o4 = o3 + POOL_WIDTH
    return u[..., :o1], u[..., o1:o2], u[..., o2:o3], u[..., o3:o4], u[..., o4:]


def context_attention(qc, kc, vc, sinks):
    scale = HEAD_DIM ** -0.5
    s = jnp.einsum('bqhgd,bkhd->bhgqk', qc, kc).astype(jnp.float32) * scale
    p = sink_softmax(s, sinks).astype(vc.dtype)
    o = jnp.einsum('bhgqk,bkhd->bqhgd', p, vc)
    return o.reshape(o.shape[0], o.shape[1], ATT_WIDTH)


def latent_attention(q, k, v, kc, vc, sinks):
    B, L = q.shape[0], q.shape[1]
    nb = L // BLOCK
    scale = HEAD_DIM ** -0.5
    qb = q.reshape(B, nb, BLOCK, N_KV_HEADS, Q_PER_KV, HEAD_DIM)
    pad = ((0, 0), (BLOCK, BLOCK), (0, 0), (0, 0))
    kp = jnp.pad(k, pad).reshape(B, nb + 2, BLOCK, N_KV_HEADS, HEAD_DIM)
    vp = jnp.pad(v, pad).reshape(B, nb + 2, BLOCK, N_KV_HEADS, HEAD_DIM)
    k_win = jnp.concatenate([kp[:, :-2], kp[:, 1:-1], kp[:, 2:]], axis=2)
    v_win = jnp.concatenate([vp[:, :-2], vp[:, 1:-1], vp[:, 2:]], axis=2)
    blk = jnp.arange(nb)
    qpos = blk[:, None] * BLOCK + jnp.arange(BLOCK)[None, :]
    kpos = blk[:, None] * BLOCK - BLOCK + jnp.arange(3 * BLOCK)[None, :]
    valid = ((jnp.abs(qpos[:, :, None] - kpos[:, None, :]) <= WINDOW)
             & (kpos >= 0)[:, None, :] & (kpos < L)[:, None, :])

    def block_fn(args):
        qi, ki, vi, mi = args
        s_loc = jnp.einsum('bqhgd,bkhd->bhgqk', qi, ki).astype(jnp.float32) * scale
        s_loc = jnp.where(mi[None, None, None], s_loc, NEG_INF)
        s_ctx = jnp.einsum('bqhgd,bkhd->bhgqk', qi, kc).astype(jnp.float32) * scale
        p = sink_softmax(jnp.concatenate([s_loc, s_ctx], axis=-1), sinks).astype(vi.dtype)
        return (jnp.einsum('bhgqk,bkhd->bqhgd', p[..., :3 * BLOCK], vi)
                + jnp.einsum('bhgqk,bkhd->bqhgd', p[..., 3 * BLOCK:], vc))

    o = lax.map(block_fn, (jnp.moveaxis(qb, 1, 0), jnp.moveaxis(k_win, 1, 0),
                           jnp.moveaxis(v_win, 1, 0), valid))
    return jnp.moveaxis(o, 0, 1).reshape(B, L, ATT_WIDTH)


def multiscale_pool(p, w_grp, pool_scale):
    B, L, _ = p.shape
    pf = p.astype(jnp.float32)
    cs = jnp.concatenate([jnp.zeros((B, 1, POOL_WIDTH), jnp.float32), jnp.cumsum(pf, axis=1)], axis=1)
    t = jnp.arange(L)
    means = []
    for gi, w in enumerate(POOL_WINDOWS):
        lo = jnp.clip(t - w // 2, 0, L)
        hi = jnp.clip(t + w // 2, 0, L)
        csg = cs[..., gi * POOL_GROUP:(gi + 1) * POOL_GROUP]
        sums = jnp.take(csg, hi, axis=1) - jnp.take(csg, lo, axis=1)
        means.append(sums / (hi - lo).astype(jnp.float32)[None, :, None])
    pooled = (jnp.concatenate(means, axis=-1) - pf).astype(p.dtype)
    pooled = pooled.reshape(B, L, N_POOL_GROUPS, POOL_GROUP)
    mixed = jnp.einsum('blgc,gcd->blgd', pooled, w_grp).reshape(B, L, POOL_WIDTH)
    return mixed * pool_scale


def merge_branches(att, pool, gate_logits, w_att_o, w_pool_o, w_out):
    g = jax.nn.sigmoid(gate_logits).reshape(gate_logits.shape[:-1] + (N_BRANCHES, D_MODEL))
    merged = g[..., 0, :] * (att @ w_att_o) + g[..., 1, :] * (pool @ w_pool_o)
    return merged @ w_out


def swiglu(h, wg, wu, wd):
    return (jax.nn.silu(h @ wg) * (h @ wu)) @ wd


def moe_ffn(h, w_router, b_router, wg, wu, wd):
    logits = (h @ w_router).astype(jnp.float32) + b_router.astype(jnp.float32)
    top_val, top_idx = lax.top_k(logits, TOP_K)
    top_w = jax.nn.softmax(top_val, axis=-1)
    combine = jnp.sum(jax.nn.one_hot(top_idx, N_EXPERTS, dtype=jnp.float32) * top_w[..., None], axis=-2)
    combine = combine.astype(h.dtype)
    out = jnp.zeros_like(h)
    for e in range(N_EXPERTS):
        out = out + combine[..., e:e + 1] * swiglu(h, wg[e], wu[e], wd[e])
    return out


def setup_inputs(seed: int = 0) -> dict:
    key = jax.random.key(seed)
    ks = jax.random.split(key, 24)
    f32 = jnp.float32
    n = lambda k, shape, s: jax.random.normal(k, shape, f32) * s
    D = D_MODEL
    return {
        "x": n(ks[0], (BATCH, SEQ, D), 1.0),
        "c": n(ks[1], (BATCH, D), 1.0),
        "ctx": n(ks[2], (BATCH, CTX_LEN, D), 1.0),
        "c_ctx": n(ks[3], (D,), 1.0),
        "w_mod": n(ks[4], (DEPTH, D, N_MOD * D), 0.5 * D ** -0.5),
        "b_mod": n(ks[5], (DEPTH, N_MOD * D), 0.02),
        "norm_g": 1.0 + n(ks[6], (DEPTH, 4, D), 0.02),
        "w_in": n(ks[7], (DEPTH, D, IN_WIDTH), D ** -0.5),
        "sinks": n(ks[8], (DEPTH, N_Q_HEADS), 0.5),
        "w_att_o": n(ks[9], (DEPTH, ATT_WIDTH, D), ATT_WIDTH ** -0.5),
        "pool_w": n(ks[10], (DEPTH, N_POOL_GROUPS, POOL_GROUP, POOL_GROUP), POOL_GROUP ** -0.5),
        "pool_scale": 1.0 + n(ks[11], (DEPTH, POOL_WIDTH), 0.1),
        "w_pool_o": n(ks[12], (DEPTH, POOL_WIDTH, D), POOL_WIDTH ** -0.5),
        "w_out": n(ks[13], (DEPTH, D, D), D ** -0.5),
        "w_ffn_gate": n(ks[14], (N_DENSE, D, D_FF), D ** -0.5),
        "w_ffn_up": n(ks[15], (N_DENSE, D, D_FF), D ** -0.5),
        "w_ffn_down": n(ks[16], (N_DENSE, D_FF, D), D_FF ** -0.5),
        "w_router": n(ks[17], (N_MOE, D, N_EXPERTS), D ** -0.5),
        "b_router": n(ks[18], (N_MOE, N_EXPERTS), 0.01),
        "w_exp_gate": n(ks[19], (N_MOE, N_EXPERTS, D, D_FF_EXPERT), D ** -0.5),
        "w_exp_up": n(ks[20], (N_MOE, N_EXPERTS, D, D_FF_EXPERT), D ** -0.5),
        "w_exp_down": n(ks[21], (N_MOE, N_EXPERTS, D_FF_EXPERT, D), D_FF_EXPERT ** -0.5),
    }


def reference(x, c, ctx, c_ctx, w_mod, b_mod, norm_g, w_in, sinks, w_att_o, pool_w, pool_scale,
              w_pool_o, w_out, w_ffn_gate, w_ffn_up, w_ffn_down, w_router, b_router,
              w_exp_gate, w_exp_up, w_exp_down):
    B, L, _ = x.shape
    C = ctx.shape[1]
    rows = L // GRID_W
    row = jnp.repeat(jnp.arange(rows), GRID_W)
    col = jnp.tile(jnp.arange(GRID_W), rows)
    sc = jax.nn.silu(c)
    sc_ctx = jax.nn.silu(c_ctx)

    def ffn(h, l):
        if l % 2 == 0:
            i = l // 2
            return swiglu(h, w_ffn_gate[i], w_ffn_up[i], w_ffn_down[i])
        i = l // 2
        return moe_ffn(h, w_router[i], b_router[i], w_exp_gate[i], w_exp_up[i], w_exp_down[i])

    for l in range(DEPTH):
        last = l == DEPTH - 1
        mod_lat = (sc @ w_mod[l] + b_mod[l])[:, None, :]
        mod_ctx = sc_ctx @ w_mod[l] + b_mod[l]
        sa, ca, ga, sf, cf, gf = jnp.split(mod_lat, N_MOD, axis=-1)
        sa_c, ca_c, ga_c, sf_c, cf_c, gf_c = jnp.split(mod_ctx, N_MOD, axis=-1)

        h = modulate(rms_norm(x, norm_g[l, 0]), sa, ca)
        hc = modulate(rms_norm(ctx, norm_g[l, 0]), sa_c, ca_c)
        q, k, v, p, gl = split_proj(h @ w_in[l])
        qc, kc, vc, pc, glc = split_proj(hc @ w_in[l])
        q = rope_2d(q.reshape(B, L, N_Q_HEADS, HEAD_DIM), row, col)
        k = rope_2d(k.reshape(B, L, N_KV_HEADS, HEAD_DIM), row, col)
        q = q.reshape(B, L, N_KV_HEADS, Q_PER_KV, HEAD_DIM)
        v = v.reshape(B, L, N_KV_HEADS, HEAD_DIM)
        kc = kc.reshape(B, C, N_KV_HEADS, HEAD_DIM)
        vc = vc.reshape(B, C, N_KV_HEADS, HEAD_DIM)
        att = latent_attention(q, k, v, kc, vc, sinks[l])
        pool = multiscale_pool(p, pool_w[l], pool_scale[l])
        y = merge_branches(att, pool, gl, w_att_o[l], w_pool_o[l], w_out[l])
        x = x + ga * rms_norm(y, norm_g[l, 1])

        hf = modulate(rms_norm(x, norm_g[l, 2]), sf, cf)
        x = x + gf * rms_norm(ffn(hf, l), norm_g[l, 3])

        if not last:
            qc = qc.reshape(B, C, N_KV_HEADS, Q_PER_KV, HEAD_DIM)
            att_c = context_attention(qc, kc, vc, sinks[l])
            pool_c = multiscale_pool(pc, pool_w[l], pool_scale[l])
            yc = merge_branches(att_c, pool_c, glc, w_att_o[l], w_pool_o[l], w_out[l])
            ctx = ctx + ga_c * rms_norm(yc, norm_g[l, 1])
            hfc = modulate(rms_norm(ctx, norm_g[l, 2]), sf_c, cf_c)
            ctx = ctx + gf_c * rms_norm(ffn(hfc, l), norm_g[l, 3])
    return x
```

```python
import functools

import jax
import jax.numpy as jnp
from jax import lax
from jax.experimental import pallas as pl
from jax.experimental.pallas import tpu as pltpu

GRID_W = 64
HEAD_DIM = 64
N_Q_HEADS = 8
N_KV_HEADS = 2
Q_PER_KV = N_Q_HEADS // N_KV_HEADS
ATT_WIDTH = N_Q_HEADS * HEAD_DIM
KV_WIDTH = N_KV_HEADS * HEAD_DIM
WINDOW = 128
BLOCK = 128
POOL_WINDOWS = (2, 4, 8, 16)
N_POOL_GROUPS = 4
POOL_GROUP = 128
POOL_WIDTH = N_POOL_GROUPS * POOL_GROUP
POOL_HALO = 8
N_MOD = 6
N_EXPERTS = 8
ROPE_BASE = 10000.0
EPS = 1e-6
NEG_INF = -1e30
LANES = 128
QKVP_WIDTH = ATT_WIDTH + 2 * KV_WIDTH + POOL_WIDTH

TOKEN_TILE = 512
EXPERT_TILE = 512
FF_SUB = 512
VMEM_LIMIT = 56 * 1024 * 1024

HEAD_ORDER = (0, 4, 1, 5, 2, 6, 3, 7)

f32 = jnp.float32
bf16 = jnp.bfloat16


def _params(sem, vmem=VMEM_LIMIT):
    return pltpu.CompilerParams(dimension_semantics=sem, vmem_limit_bytes=vmem)


def _rms(x, g):
    ms = jnp.mean(x * x, axis=-1, keepdims=True)
    return x * lax.rsqrt(ms + EPS) * g


def _mod_kernel(c_ref, w_ref, b_ref, o_ref):
    c = c_ref[...]
    sc = (c * jax.nn.sigmoid(c)).astype(bf16)
    o_ref[0] = jnp.dot(sc, w_ref[0].astype(bf16), preferred_element_type=f32) + b_ref[0]


def _modulation(c8, w_mod, b_mod):
    depth, d, n = w_mod.shape
    tn = n // 4
    return pl.pallas_call(
        _mod_kernel,
        out_shape=jax.ShapeDtypeStruct((depth, 8, n), f32),
        grid=(depth, n // tn),
        in_specs=[
            pl.BlockSpec((8, d), lambda l, j: (0, 0)),
            pl.BlockSpec((1, d, tn), lambda l, j: (l, 0, j)),
            pl.BlockSpec((1, 1, tn), lambda l, j: (l, 0, j)),
        ],
        out_specs=pl.BlockSpec((1, 8, tn), lambda l, j: (l, 0, j)),
        compiler_params=_params(("arbitrary", "arbitrary")),
        name="modulation",
    )(c8, w_mod, b_mod.reshape(depth, 1, n))


def _inproj_kernel(x_ref, mod_ref, g_ref, cos_ref, sin_ref, w_ref, q_ref, k_ref, v_ref, p_ref,
                   *, n_lat_tiles):
    i = pl.program_id(0)
    m = mod_ref[0]
    h = _rms(x_ref[...], g_ref[0, 0:1, :])
    h = h * (1.0 + m[1:2, :]) + m[0:1, :]
    u = jnp.dot(h.astype(bf16), w_ref[...], preferred_element_type=f32)

    is_lat = i < n_lat_tiles
    cos = jnp.where(is_lat, cos_ref[...], 1.0)
    sin = jnp.where(is_lat, sin_ref[...], 0.0)
    lane = lax.broadcasted_iota(jnp.int32, (1, LANES), 1)
    first_half = (lane % 32) < 16

    def rope(t):
        partner = jnp.where(first_half, pltpu.roll(t, LANES - 16, axis=1), pltpu.roll(t, 16, axis=1))
        return t * cos + partner * sin

    for jb in range(ATT_WIDTH // LANES):
        blk = rope(u[:, jb * LANES:(jb + 1) * LANES])
        q_ref[:, jb * LANES:(jb + 1) * LANES] = (blk * (HEAD_DIM ** -0.5)).astype(bf16)
    o = ATT_WIDTH
    k_ref[...] = rope(u[:, o:o + KV_WIDTH]).astype(bf16)
    o += KV_WIDTH
    v_ref[...] = u[:, o:o + KV_WIDTH].astype(bf16)
    o += KV_WIDTH
    p_ref[...] = u[:, o:o + POOL_WIDTH]


def _inproj(x, mod_l, g_l, cos, sin, w, *, n_lat_tiles, tiles_per_batch, n_batch):
    t_all, d = x.shape
    tm = TOKEN_TILE
    n_tiles = t_all // tm

    def mod_idx(i):
        return (jnp.minimum(i // tiles_per_batch, n_batch), 0, 0)

    def tab_idx(i):
        return (jnp.where(i < n_lat_tiles, i % tiles_per_batch, 0), 0)

    return pl.pallas_call(
        functools.partial(_inproj_kernel, n_lat_tiles=n_lat_tiles),
        out_shape=(
            jax.ShapeDtypeStruct((t_all, ATT_WIDTH), bf16),
            jax.ShapeDtypeStruct((t_all, KV_WIDTH), bf16),
            jax.ShapeDtypeStruct((t_all, KV_WIDTH), bf16),
            jax.ShapeDtypeStruct((t_all, POOL_WIDTH), f32),
        ),
        grid=(n_tiles,),
        in_specs=[
            pl.BlockSpec((tm, d), lambda i: (i, 0)),
            pl.BlockSpec((1, N_MOD, d), mod_idx),
            pl.BlockSpec((1, 4, d), lambda i: (0, 0, 0)),
            pl.BlockSpec((tm, LANES), tab_idx),
            pl.BlockSpec((tm, LANES), tab_idx),
            pl.BlockSpec((d, QKVP_WIDTH), lambda i: (0, 0)),
        ],
        out_specs=(
            pl.BlockSpec((tm, ATT_WIDTH), lambda i: (i, 0)),
            pl.BlockSpec((tm, KV_WIDTH), lambda i: (i, 0)),
            pl.BlockSpec((tm, KV_WIDTH), lambda i: (i, 0)),
            pl.BlockSpec((tm, POOL_WIDTH), lambda i: (i, 0)),
        ),
        compiler_params=_params(("parallel",)),
        name="inproj",
    )(x, mod_l, g_l, cos, sin, w)


def _attn_kernel(sink_ref, q_ref, kp_ref, kc_ref, kn_ref, kx_ref, vp_ref, vc_ref, vn_ref, vx_ref,
                 o_ref, *, nb, n_ctx):
    i = pl.program_id(1)
    n_loc = 3 * BLOCK
    nk = n_loc + n_ctx

    lane = lax.broadcasted_iota(jnp.int32, (1, LANES), 1)
    lo_half = lane < HEAD_DIM

    def split_heads(a, b, c, x):
        cat = jnp.concatenate([a[...], b[...], c[...], x[...]], axis=0)
        zero = jnp.zeros_like(cat)
        return jnp.concatenate([jnp.where(lo_half, cat, zero), jnp.where(lo_half, zero, cat)], axis=0)

    kk = split_heads(kp_ref, kc_ref, kn_ref, kx_ref)
    vv = split_heads(vp_ref, vc_ref, vn_ref, vx_ref)

    r = lax.broadcasted_iota(jnp.int32, (BLOCK, n_loc), 0)
    c = lax.broadcasted_iota(jnp.int32, (BLOCK, n_loc), 1)
    c_lo = jnp.where(i < nb, jnp.where(i >= 1, 0, BLOCK), n_loc)
    c_hi = jnp.where(i <= nb - 2, n_loc, 2 * BLOCK)
    ok = (jnp.abs(r + BLOCK - c) <= WINDOW) & (c >= c_lo) & (c < c_hi)
    bias_loc = jnp.where(ok, 0.0, NEG_INF)
    bias_ctx = jnp.zeros((BLOCK, n_ctx), f32)
    bias = jnp.concatenate([bias_loc, bias_ctx, bias_loc, bias_ctx], axis=1)

    for j in range(ATT_WIDTH // LANES):
        qj = q_ref[:, j * LANES:(j + 1) * LANES]
        s = lax.dot_general(qj, kk, (((1,), (1,)), ((), ())), preferred_element_type=f32)
        s = s + bias
        ps = []
        for half in range(2):
            sh = s[:, half * nk:(half + 1) * nk]
            sink = sink_ref[2 * j + half]
            mx = jnp.maximum(jnp.max(sh, axis=1, keepdims=True), sink)
            e = jnp.exp(sh - mx)
            denom = jnp.sum(e, axis=1, keepdims=True) + jnp.exp(sink - mx)
            ps.append((e / denom).astype(bf16))
        pcat = jnp.concatenate(ps, axis=1)
        o = jnp.dot(pcat, vv, preferred_element_type=f32)
        o_ref[:, j * LANES:(j + 1) * LANES] = o.astype(bf16)


def _attention(sinks, q, k, v, *, n_batch, seq, n_ctx, with_ctx):
    nb = seq // BLOCK
    cb = n_ctx // BLOCK
    n_steps = nb + (cb if with_ctx else 0)
    t_out = n_batch * (seq + (n_ctx if with_ctx else 0))
    ctx_blk0 = (n_batch * seq) // n_ctx

    def q_idx(b, i):
        return (jnp.where(i < nb, b * nb + i, n_batch * nb + b * cb + (i - nb)), 0)

    def prev_idx(b, i):
        return (b * nb + jnp.clip(i - 1, 0, nb - 1), 0)

    def cur_idx(b, i):
        return (b * nb + jnp.minimum(i, nb - 1), 0)

    def next_idx(b, i):
        return (b * nb + jnp.minimum(i + 1, nb - 1), 0)

    def ctx_idx(b, i):
        return (ctx_blk0 + b, 0)

    kv_specs = [
        pl.BlockSpec((BLOCK, KV_WIDTH), prev_idx),
        pl.BlockSpec((BLOCK, KV_WIDTH), cur_idx),
        pl.BlockSpec((BLOCK, KV_WIDTH), next_idx),
        pl.BlockSpec((n_ctx, KV_WIDTH), ctx_idx),
    ]
    return pl.pallas_call(
        functools.partial(_attn_kernel, nb=nb, n_ctx=n_ctx),
        out_shape=jax.ShapeDtypeStruct((t_out, ATT_WIDTH), bf16),
        grid=(n_batch, n_steps),
        in_specs=[pl.BlockSpec(memory_space=pltpu.SMEM),
                  pl.BlockSpec((BLOCK, ATT_WIDTH), q_idx)] + kv_specs + kv_specs,
        out_specs=pl.BlockSpec((BLOCK, ATT_WIDTH), q_idx),
        compiler_params=_params(("parallel", "parallel")),
        name="attention",
    )(sinks, q, k, k, k, k, v, v, v, v)


def _merge_kernel(x_ref, att_ref, pp_ref, pc_ref, pn_ref, mod_ref, g_ref, wgate_ref, watt_ref,
                  wpool_ref, wout_ref, poolw_ref, pscale_ref, *rest,
                  n_lat_tiles, tiles_per_batch, seq, n_ctx, with_router, hf_dtype):
    if with_router:
        wr_ref, br_ref, x1_ref, hf_ref, lg_ref, p_scr = rest
    else:
        x1_ref, hf_ref, p_scr = rest
    i = pl.program_id(0)
    tm = x_ref.shape[0]
    d = x_ref.shape[1]
    m = mod_ref[0]
    g = g_ref[0]
    x = x_ref[...]

    h = _rms(x, g[0:1, :])
    h = (h * (1.0 + m[1:2, :]) + m[0:1, :]).astype(bf16)
    gate = jax.nn.sigmoid(jnp.dot(h, wgate_ref[...], preferred_element_type=f32))

    p_scr[0:POOL_HALO, :] = pp_ref[...]
    p_scr[POOL_HALO:POOL_HALO + tm, :] = pc_ref[...]
    p_scr[POOL_HALO + tm:POOL_HALO + tm + POOL_HALO, :] = pn_ref[...]
    r = lax.broadcasted_iota(jnp.int32, (tm, 1), 0)
    is_lat = i < n_lat_tiles
    pos = jnp.where(is_lat, (i % tiles_per_batch) * tm + r, ((i - n_lat_tiles) * tm + r) % n_ctx)
    seq_len = jnp.where(is_lat, seq, n_ctx)
    mixed = []
    for gi, w in enumerate(POOL_WINDOWS):
        lanes = slice(gi * POOL_GROUP, (gi + 1) * POOL_GROUP)
        acc = jnp.zeros((tm, POOL_GROUP), f32)
        for s in range(-(w // 2), w // 2):
            blk = p_scr[POOL_HALO + s:POOL_HALO + s + tm, lanes]
            if s < 0:
                blk = jnp.where(pos >= -s, blk, 0.0)
            elif s > 0:
                blk = jnp.where(pos < seq_len - s, blk, 0.0)
            acc = acc + blk
        cnt = (jnp.minimum(pos + w // 2, seq_len) - jnp.maximum(pos - w // 2, 0)).astype(f32)
        pooled = (acc / cnt - pc_ref[:, lanes]).astype(bf16)
        mixed.append((jnp.dot(pooled, poolw_ref[gi], preferred_element_type=f32)
                      * pscale_ref[:, lanes]).astype(bf16))
    pool_acc = jnp.dot(jnp.concatenate(mixed, axis=1), wpool_ref[...], preferred_element_type=f32)

    a = jnp.dot(att_ref[...], watt_ref[...], preferred_element_type=f32)
    merged = (gate[:, :d] * a + gate[:, d:] * pool_acc).astype(bf16)
    y = jnp.dot(merged, wout_ref[...], preferred_element_type=f32)
    x1 = x + m[2:3, :] * _rms(y, g[1:2, :])
    x1_ref[...] = x1
    hf = _rms(x1, g[2:3, :]) * (1.0 + m[4:5, :]) + m[3:4, :]
    hf_ref[...] = hf.astype(hf_dtype)
    if with_router:
        lg_ref[...] = jnp.dot(hf.astype(bf16), wr_ref[...], preferred_element_type=f32) + br_ref[...]


def _merge(x, att, p, mod_l, g_l, wgate, watt, wpool, wout, poolw, pscale, router,
           *, n_rows, n_lat_tiles, tiles_per_batch, n_batch, seq, n_ctx, hf_dtype):
    d = x.shape[1]
    tm = TOKEN_TILE
    n_tiles = n_rows // tm
    hb = tm // POOL_HALO
    last_halo = p.shape[0] // POOL_HALO - 1
    with_router = router is not None

    def mod_idx(i):
        return (jnp.minimum(i // tiles_per_batch, n_batch), 0, 0)

    const2 = lambda i: (0, 0)
    in_specs = [
        pl.BlockSpec((tm, d), lambda i: (i, 0)),
        pl.BlockSpec((tm, ATT_WIDTH), lambda i: (i, 0)),
        pl.BlockSpec((POOL_HALO, POOL_WIDTH), lambda i: (jnp.maximum(i * hb - 1, 0), 0)),
        pl.BlockSpec((tm, POOL_WIDTH), lambda i: (i, 0)),
        pl.BlockSpec((POOL_HALO, POOL_WIDTH), lambda i: (jnp.minimum((i + 1) * hb, last_halo), 0)),
        pl.BlockSpec((1, N_MOD, d), mod_idx),
        pl.BlockSpec((1, 4, d), lambda i: (0, 0, 0)),
        pl.BlockSpec(wgate.shape, const2),
        pl.BlockSpec(watt.shape, const2),
        pl.BlockSpec(wpool.shape, const2),
        pl.BlockSpec(wout.shape, const2),
        pl.BlockSpec(poolw.shape, lambda i: (0, 0, 0)),
        pl.BlockSpec(pscale.shape, const2),
    ]
    args = [x, att, p, p, p, mod_l, g_l, wgate, watt, wpool, wout, poolw, pscale]
    out_shape = [jax.ShapeDtypeStruct((n_rows, d), f32), jax.ShapeDtypeStruct((n_rows, d), hf_dtype)]
    out_specs = [pl.BlockSpec((tm, d), lambda i: (i, 0)), pl.BlockSpec((tm, d), lambda i: (i, 0))]
    if with_router:
        wr, br = router
        in_specs += [pl.BlockSpec(wr.shape, const2), pl.BlockSpec(br.shape, const2)]
        args += [wr, br]
        out_shape.append(jax.ShapeDtypeStruct((n_rows, LANES), f32))
        out_specs.append(pl.BlockSpec((tm, LANES), lambda i: (i, 0)))
    return pl.pallas_call(
        functools.partial(_merge_kernel, n_lat_tiles=n_lat_tiles, tiles_per_batch=tiles_per_batch,
                          seq=seq, n_ctx=n_ctx, with_router=with_router, hf_dtype=hf_dtype),
        out_shape=tuple(out_shape),
        grid=(n_tiles,),
        in_specs=in_specs,
        out_specs=tuple(out_specs),
        scratch_shapes=[pltpu.VMEM((tm + 2 * POOL_HALO, POOL_WIDTH), f32)],
        compiler_params=_params(("parallel",)),
        name="merge",
    )(*args)


def _ff_splits(fc):
    edges = list(range(0, fc, FF_SUB)) + [fc]
    return tuple(zip(edges[:-1], edges[1:]))


def _ffn_kernel(te_ref, nv_ref, x_ref, wg_ref, wu_ref, wd_ref, *rest, n_chunks, fuse_residual):
    if fuse_residual:
        xres_ref, mod_ref, g_ref, o_ref, acc_ref = rest
    else:
        o_ref, acc_ref = rest
    i = pl.program_id(0)
    j = pl.program_id(1)

    @pl.when(i < nv_ref[0])
    def _():
        @pl.when(j == 0)
        def _():
            acc_ref[...] = jnp.zeros_like(acc_ref)

        xb = x_ref[...].astype(bf16)
        for c0, c1 in _ff_splits(wg_ref.shape[1]):
            gt = jnp.dot(xb, wg_ref[:, c0:c1], preferred_element_type=f32)
            up = jnp.dot(xb, wu_ref[:, c0:c1], preferred_element_type=f32)
            act = (gt * jax.nn.sigmoid(gt) * up).astype(bf16)
            acc_ref[...] += jnp.dot(act, wd_ref[c0:c1, :], preferred_element_type=f32)

        @pl.when(j == n_chunks - 1)
        def _():
            y = acc_ref[...]
            if fuse_residual:
                m = mod_ref[0]
                y = xres_ref[...] + m[5:6, :] * _rms(y, g_ref[0, 3:4, :])
            o_ref[...] = y

    @pl.when(jnp.logical_and(i >= nv_ref[0], j == n_chunks - 1))
    def _():
        o_ref[...] = jnp.zeros_like(o_ref)


def _ffn(tile_expert, n_valid, x, wg, wu, wd, residual, *, tm, n_chunks):
    n_rows, d = x.shape
    n_tiles = n_rows // tm
    ff = wg.shape[2]
    fc = ff // n_chunks
    fuse = residual is not None

    def row_idx(i, j, te, nv):
        return (jnp.minimum(i, nv[0] - 1), 0)

    def w_in_idx(i, j, te, nv):
        return (te[jnp.minimum(i, nv[0] - 1)], 0, j)

    def w_out_idx(i, j, te, nv):
        return (te[jnp.minimum(i, nv[0] - 1)], j, 0)

    in_specs = [
        pl.BlockSpec((tm, d), row_idx),
        pl.BlockSpec((None, d, fc), w_in_idx),
        pl.BlockSpec((None, d, fc), w_in_idx),
        pl.BlockSpec((None, fc, d), w_out_idx),
    ]
    args = [x, wg, wu, wd]
    if fuse:
        xres, mod_l, g_l, tiles_per_batch, n_batch = residual
        in_specs += [
            pl.BlockSpec((tm, d), row_idx),
            pl.BlockSpec((1, N_MOD, d),
                         lambda i, j, te, nv: (jnp.minimum(i // tiles_per_batch, n_batch), 0, 0)),
            pl.BlockSpec((1, 4, d), lambda i, j, te, nv: (0, 0, 0)),
        ]
        args += [xres, mod_l, g_l]
    return pl.pallas_call(
        functools.partial(_ffn_kernel, n_chunks=n_chunks, fuse_residual=fuse),
        out_shape=jax.ShapeDtypeStruct((n_rows, d), f32),
        grid_spec=pltpu.PrefetchScalarGridSpec(
            num_scalar_prefetch=2,
            grid=(n_tiles, n_chunks),
            in_specs=in_specs,
            out_specs=pl.BlockSpec((tm, d), lambda i, j, te, nv: (i, 0)),
            scratch_shapes=[pltpu.VMEM((tm, d), f32)],
        ),
        compiler_params=_params(("arbitrary", "arbitrary")),
        name="ffn",
    )(tile_expert, n_valid, *args)


def _route_kernel(lg_ref, out_ref, cnt_ref, carry_ref):
    i = pl.program_id(0)
    tm = lg_ref.shape[0]

    @pl.when(i == 0)
    def _():
        carry_ref[...] = jnp.zeros_like(carry_ref)

    lane = lax.broadcasted_iota(jnp.int32, (tm, LANES), 1).astype(f32)
    lg = jnp.where(lane < N_EXPERTS, lg_ref[...], -jnp.inf)
    m1 = jnp.max(lg, axis=1, keepdims=True)
    i1 = jnp.min(jnp.where(lg == m1, lane, float(LANES)), axis=1, keepdims=True)
    lg2 = jnp.where(lane == i1, -jnp.inf, lg)
    m2 = jnp.max(lg2, axis=1, keepdims=True)
    i2 = jnp.min(jnp.where(lg2 == m2, lane, float(LANES)), axis=1, keepdims=True)
    e2 = jnp.exp(m2 - m1)
    w1 = 1.0 / (1.0 + e2)
    w2 = e2 / (1.0 + e2)

    ind = jnp.where(lane == i1, 1.0, jnp.where(lane == i2, 1.0, 0.0))
    rr = lax.broadcasted_iota(jnp.int32, (tm, tm), 0)
    cc = lax.broadcasted_iota(jnp.int32, (tm, tm), 1)
    tri = jnp.where(rr > cc, 1.0, 0.0).astype(bf16)
    rank = jnp.dot(tri, ind.astype(bf16), preferred_element_type=f32) + carry_ref[0:1, :]
    r1 = jnp.sum(jnp.where(lane == i1, rank, 0.0), axis=1, keepdims=True)
    r2 = jnp.sum(jnp.where(lane == i2, rank, 0.0), axis=1, keepdims=True)

    fields = (i1, i2, r1, r2, w1, w2)
    packed = jnp.zeros((tm, LANES), f32)
    for n, fld in enumerate(fields):
        packed = jnp.where(lane == float(n), fld, packed)
    out_ref[...] = packed

    carry_ref[...] = carry_ref[...] + jnp.sum(ind, axis=0, keepdims=True)
    cnt_ref[...] = carry_ref[...]


def _route(logits):
    n_rows = logits.shape[0]
    tm = TOKEN_TILE
    return pl.pallas_call(
        _route_kernel,
        out_shape=(jax.ShapeDtypeStruct((n_rows, LANES), f32), jax.ShapeDtypeStruct((8, LANES), f32)),
        grid=(n_rows // tm,),
        in_specs=[pl.BlockSpec((tm, LANES), lambda i: (i, 0))],
        out_specs=(pl.BlockSpec((tm, LANES), lambda i: (i, 0)), pl.BlockSpec((8, LANES), lambda i: (0, 0))),
        scratch_shapes=[pltpu.VMEM((8, LANES), f32)],
        compiler_params=_params(("arbitrary",)),
        name="route",
    )(logits)


def _row_copy(src_ref, src_row, dst_ref, dst_row, sem):
    return pltpu.make_async_copy(src_ref.at[pl.ds(src_row, 1), :], dst_ref.at[pl.ds(dst_row, 1), :], sem)


def _dispatch_kernel(pos_ref, hf_ref, xs_in_ref, xs_ref, sem):
    del xs_in_ref
    tm = hf_ref.shape[0]

    def start(r, carry):
        _row_copy(hf_ref, r, xs_ref, pos_ref[0, 0, 2 * r], sem).start()
        _row_copy(hf_ref, r, xs_ref, pos_ref[0, 0, 2 * r + 1], sem).start()
        return carry

    lax.fori_loop(0, tm, start, 0)

    def wait(r, carry):
        _row_copy(hf_ref, r, xs_ref, pos_ref[0, 0, 2 * r], sem).wait()
        _row_copy(hf_ref, r, xs_ref, pos_ref[0, 0, 2 * r + 1], sem).wait()
        return carry

    lax.fori_loop(0, tm, wait, 0)


def _dispatch(pos, hf, xs_init):
    n_rows, d = hf.shape
    tm = TOKEN_TILE
    n_tiles = n_rows // tm
    return pl.pallas_call(
        _dispatch_kernel,
        out_shape=jax.ShapeDtypeStruct(xs_init.shape, xs_init.dtype),
        grid=(n_tiles,),
        in_specs=[
            pl.BlockSpec((1, 1, 2 * tm), lambda i: (i, 0, 0), memory_space=pltpu.SMEM),
            pl.BlockSpec((tm, d), lambda i: (i, 0)),
            pl.BlockSpec(memory_space=pl.ANY),
        ],
        out_specs=pl.BlockSpec(memory_space=pl.ANY),
        scratch_shapes=[pltpu.SemaphoreType.DMA(())],
        input_output_aliases={2: 0},
        compiler_params=_params(("arbitrary",)),
        name="dispatch",
    )(pos.reshape(n_tiles, 1, 2 * tm), hf, xs_init)


def _combine_kernel(pos_ref, y_ref, rt_ref, x_ref, mod_ref, g_ref, o_ref, buf0, buf1, sem):
    tm = x_ref.shape[0]

    def start(r, carry):
        _row_copy(y_ref, pos_ref[0, 0, 2 * r], buf0, r, sem).start()
        _row_copy(y_ref, pos_ref[0, 0, 2 * r + 1], buf1, r, sem).start()
        return carry

    lax.fori_loop(0, tm, start, 0)

    def wait(r, carry):
        _row_copy(y_ref, pos_ref[0, 0, 2 * r], buf0, r, sem).wait()
        _row_copy(y_ref, pos_ref[0, 0, 2 * r + 1], buf1, r, sem).wait()
        return carry

    lax.fori_loop(0, tm, wait, 0)

    rt = rt_ref[...]
    y = rt[:, 4:5] * buf0[...] + rt[:, 5:6] * buf1[...]
    m = mod_ref[0]
    o_ref[...] = x_ref[...] + m[5:6, :] * _rms(y, g_ref[0, 3:4, :])


def _combine(pos, y, routed, x, mod_l, g_l, *, tiles_per_batch, n_batch):
    n_rows, d = x.shape
    tm = TOKEN_TILE
    n_tiles = n_rows // tm
    return pl.pallas_call(
        _combine_kernel,
        out_shape=jax.ShapeDtypeStruct((n_rows, d), f32),
        grid=(n_tiles,),
        in_specs=[
            pl.BlockSpec((1, 1, 2 * tm), lambda i: (i, 0, 0), memory_space=pltpu.SMEM),
            pl.BlockSpec(memory_space=pl.ANY),
            pl.BlockSpec((tm, LANES), lambda i: (i, 0)),
            pl.BlockSpec((tm, d), lambda i: (i, 0)),
            pl.BlockSpec((1, N_MOD, d), lambda i: (jnp.minimum(i // tiles_per_batch, n_batch), 0, 0)),
            pl.BlockSpec((1, 4, d), lambda i: (0, 0, 0)),
        ],
        out_specs=pl.BlockSpec((tm, d), lambda i: (i, 0)),
        scratch_shapes=[pltpu.VMEM((tm, d), f32), pltpu.VMEM((tm, d), f32), pltpu.SemaphoreType.DMA(())],
        compiler_params=_params(("arbitrary",)),
        name="combine",
    )(pos.reshape(n_tiles, 1, 2 * tm), y, routed, x, mod_l, g_l)


def _rope_tables(seq):
    half = HEAD_DIM // 4
    freqs = ROPE_BASE ** (-jnp.arange(half, dtype=f32) / half)
    t = jnp.arange(seq)
    row = (t // GRID_W).astype(f32)[:, None] * freqs[None, :]
    col = (t % GRID_W).astype(f32)[:, None] * freqs[None, :]
    cos = jnp.concatenate([jnp.cos(row), jnp.cos(row), jnp.cos(col), jnp.cos(col)], axis=1)
    sin = jnp.concatenate([-jnp.sin(row), jnp.sin(row), -jnp.sin(col), jnp.sin(col)], axis=1)
    reps = LANES // HEAD_DIM
    return jnp.tile(cos, (1, reps)), jnp.tile(sin, (1, reps))


def _head_perm():
    cols = []
    for hq in HEAD_ORDER:
        cols.extend(range(hq * HEAD_DIM, (hq + 1) * HEAD_DIM))
    return jnp.array(cols, dtype=jnp.int32)


def _moe_plan(routed, counts, n_tiles_max):
    te_rows = EXPERT_TILE
    cnt = counts[0, :N_EXPERTS].astype(jnp.int32)
    padded = ((cnt + te_rows - 1) // te_rows) * te_rows
    ends = jnp.cumsum(padded)
    starts = ends - padded
    idx = routed[:, 0:2].astype(jnp.int32)
    rank = routed[:, 2:4].astype(jnp.int32)
    onehot = idx[:, :, None] == jnp.arange(N_EXPERTS, dtype=jnp.int32)[None, None, :]
    pos = rank + jnp.sum(jnp.where(onehot, starts[None, None, :], 0), axis=-1)
    tile_start = jnp.arange(n_tiles_max, dtype=jnp.int32) * te_rows
    tile_expert = jnp.minimum(
        jnp.sum((tile_start[:, None] >= ends[None, :]).astype(jnp.int32), axis=1), N_EXPERTS - 1)
    n_valid = (ends[-1] // te_rows).astype(jnp.int32).reshape(1)
    return pos.reshape(-1), tile_expert.astype(jnp.int32), n_valid


def kernel(x, c, ctx, c_ctx, w_mod, b_mod, norm_g, w_in, sinks, w_att_o, pool_w, pool_scale, w_pool_o,
           w_out, w_ffn_gate, w_ffn_up, w_ffn_down, w_router, b_router, w_exp_gate, w_exp_up,
           w_exp_down):
    n_batch, seq, d = x.shape
    n_ctx = ctx.shape[1]
    depth = w_mod.shape[0]
    tm = TOKEN_TILE
    n_lat = n_batch * seq
    t_all = n_lat + n_batch * n_ctx
    assert seq % tm == 0 and (n_batch * n_ctx) % tm == 0 and n_lat % n_ctx == 0
    assert (n_ctx % tm == 0) or (tm % n_ctx == 0)
    n_lat_tiles = n_lat // tm
    tiles_per_batch = seq // tm
    geom = dict(n_lat_tiles=n_lat_tiles, tiles_per_batch=tiles_per_batch, n_batch=n_batch)

    c8 = jnp.zeros((8, d), f32).at[:n_batch].set(c).at[n_batch].set(c_ctx)
    mod = _modulation(c8, w_mod, b_mod).reshape(depth, 8, N_MOD, d)
    cos, sin = _rope_tables(seq)
    perm = _head_perm()

    qkvp_cols = jnp.concatenate([perm, jnp.arange(ATT_WIDTH, QKVP_WIDTH, dtype=jnp.int32)])
    w_qkvp = w_in[:, :, :QKVP_WIDTH][:, :, qkvp_cols].astype(bf16)
    w_gate = w_in[:, :, QKVP_WIDTH:].astype(bf16)
    w_att = w_att_o[:, perm, :].astype(bf16)
    w_pool = w_pool_o.astype(bf16)
    w_o = w_out.astype(bf16)
    poolw = pool_w.astype(bf16)
    sinks_p = sinks[:, jnp.array(HEAD_ORDER)]
    w_r = jnp.zeros((w_router.shape[0], d, LANES), bf16).at[:, :, :N_EXPERTS].set(w_router.astype(bf16))
    b_r = jnp.zeros((b_router.shape[0], 1, LANES), f32).at[:, 0, :N_EXPERTS].set(b_router)
    wd_g, wd_u, wd_d = (w.astype(bf16) for w in (w_ffn_gate, w_ffn_up, w_ffn_down))
    we_g, we_u, we_d = (w.astype(bf16) for w in (w_exp_gate, w_exp_up, w_exp_down))

    xa = jnp.concatenate([x.reshape(n_lat, d), ctx.reshape(n_batch * n_ctx, d)], axis=0)

    for l in range(depth):
        last = l == depth - 1
        moe = l % 2 == 1
        n_rows = n_lat if last else t_all
        mod_l = mod[l]
        g_l = norm_g[l:l + 1]

        q, k, v, p = _inproj(xa, mod_l, g_l, cos, sin, w_qkvp[l], **geom)
        att = _attention(sinks_p[l], q, k, v, n_batch=n_batch, seq=seq, n_ctx=n_ctx, with_ctx=not last)
        router = (w_r[l // 2], b_r[l // 2]) if moe else None
        outs = _merge(xa, att, p, mod_l, g_l, w_gate[l], w_att[l], w_pool[l], w_o[l], poolw[l],
                      pool_scale[l:l + 1], router, n_rows=n_rows, seq=seq, n_ctx=n_ctx,
                      hf_dtype=f32 if moe else bf16, **geom)
        if not moe:
            x1, hf = outs
            i = l // 2
            n_tiles = n_rows // tm
            xa = _ffn(jnp.zeros((n_tiles,), jnp.int32), jnp.full((1,), n_tiles, jnp.int32), hf,
                      wd_g[i:i + 1], wd_u[i:i + 1], wd_d[i:i + 1],
                      (x1, mod_l, g_l, tiles_per_batch, n_batch), tm=tm, n_chunks=2)
        else:
            x1, hf, logits = outs
            i = l // 2
            routed, counts = _route(logits)
            n_tiles_max = (2 * n_rows) // EXPERT_TILE + N_EXPERTS
            pos, tile_expert, n_valid = _moe_plan(routed, counts, n_tiles_max)
            xs = _dispatch(pos, hf, jnp.zeros((n_tiles_max * EXPERT_TILE, d), f32))
            y = _ffn(tile_expert, n_valid, xs, we_g[i], we_u[i], we_d[i], None,
                     tm=EXPERT_TILE, n_chunks=2)
            xa = _combine(pos, y, routed, x1, mod_l, g_l, tiles_per_batch=tiles_per_batch,
                          n_batch=n_batch)
    return xa[:n_lat].reshape(n_batch, seq, d)
```

```python
import functools

import jax
import jax.numpy as jnp
from jax import lax
from jax.experimental import pallas as pl
from jax.experimental.pallas import tpu as pltpu

GRID_W = 64
HEAD_DIM = 64
N_Q_HEADS = 8
N_KV_HEADS = 2
Q_PER_KV = N_Q_HEADS // N_KV_HEADS
ATT_WIDTH = N_Q_HEADS * HEAD_DIM
KV_WIDTH = N_KV_HEADS * HEAD_DIM
WINDOW = 128
BLOCK = 128
POOL_WINDOWS = (2, 4, 8, 16)
N_POOL_GROUPS = 4
POOL_GROUP = 128
POOL_WIDTH = N_POOL_GROUPS * POOL_GROUP
POOL_HALO = 8
N_MOD = 6
N_EXPERTS = 8
ROPE_BASE = 10000.0
EPS = 1e-6
NEG_INF = -1e30
LOG2_E = 1.4426950408889634
LANES = 128
QKVP_WIDTH = ATT_WIDTH + 2 * KV_WIDTH + POOL_WIDTH

TOKEN_TILE = 512
EXPERT_TILE = 512
FF_SUB = 512
ROW_DMA_UNROLL = 8
VMEM_LIMIT = 56 * 1024 * 1024

HEAD_ORDER = (0, 4, 1, 5, 2, 6, 3, 7)

f32 = jnp.float32
bf16 = jnp.bfloat16


def _params(sem, vmem=VMEM_LIMIT):
    return pltpu.CompilerParams(dimension_semantics=sem, vmem_limit_bytes=vmem)


def _rms(x, g):
    ms = jnp.mean(x * x, axis=-1, keepdims=True)
    return x * lax.rsqrt(ms + EPS) * g


def _mod_kernel(c_ref, w_ref, b_ref, o_ref):
    c = c_ref[...]
    sc = (c * jax.nn.sigmoid(c)).astype(bf16)
    o_ref[0] = jnp.dot(sc, w_ref[0].astype(bf16), preferred_element_type=f32) + b_ref[0]


def _modulation(c8, w_mod, b_mod):
    depth, d, n = w_mod.shape
    tn = n // 4
    return pl.pallas_call(
        _mod_kernel,
        out_shape=jax.ShapeDtypeStruct((depth, 8, n), f32),
        grid=(depth, n // tn),
        in_specs=[
            pl.BlockSpec((8, d), lambda l, j: (0, 0)),
            pl.BlockSpec((1, d, tn), lambda l, j: (l, 0, j)),
            pl.BlockSpec((1, 1, tn), lambda l, j: (l, 0, j)),
        ],
        out_specs=pl.BlockSpec((1, 8, tn), lambda l, j: (l, 0, j)),
        compiler_params=_params(("arbitrary", "arbitrary")),
        name="modulation",
    )(c8, w_mod, b_mod.reshape(depth, 1, n))


def _inproj_kernel(x_ref, mod_ref, g_ref, cos_ref, sin_ref, w_ref, wvt_ref, q_ref, k_ref, vt_ref, p_ref,
                   *, n_lat_tiles):
    i = pl.program_id(0)
    m = mod_ref[0]
    h = _rms(x_ref[...], g_ref[0, 0:1, :])
    h = (h * (1.0 + m[1:2, :]) + m[0:1, :]).astype(bf16)
    u = jnp.dot(h, w_ref[...], preferred_element_type=f32)
    vt = lax.dot_general(wvt_ref[...], h, (((1,), (1,)), ((), ())), preferred_element_type=f32)
    vt_ref[...] = vt.astype(bf16)

    is_lat = i < n_lat_tiles
    cos = jnp.where(is_lat, cos_ref[...], 1.0)
    sin = jnp.where(is_lat, sin_ref[...], 0.0)
    lane = lax.broadcasted_iota(jnp.int32, (1, LANES), 1)
    first_half = (lane % 32) < 16

    def rope(t):
        partner = jnp.where(first_half, pltpu.roll(t, LANES - 16, axis=1), pltpu.roll(t, 16, axis=1))
        return t * cos + partner * sin

    for jb in range(ATT_WIDTH // LANES):
        blk = rope(u[:, jb * LANES:(jb + 1) * LANES])
        q_ref[:, jb * LANES:(jb + 1) * LANES] = (blk * (LOG2_E * HEAD_DIM ** -0.5)).astype(bf16)
    o = ATT_WIDTH
    k_ref[...] = rope(u[:, o:o + KV_WIDTH]).astype(bf16)
    o += KV_WIDTH
    p_ref[...] = u[:, o:o + POOL_WIDTH]


def _inproj(x, mod_l, g_l, cos, sin, w, wvt, *, n_lat_tiles, tiles_per_batch, n_batch):
    t_all, d = x.shape
    tm = TOKEN_TILE
    n_tiles = t_all // tm

    def mod_idx(i):
        return (jnp.minimum(i // tiles_per_batch, n_batch), 0, 0)

    def tab_idx(i):
        return (jnp.where(i < n_lat_tiles, i % tiles_per_batch, 0), 0)

    return pl.pallas_call(
        functools.partial(_inproj_kernel, n_lat_tiles=n_lat_tiles),
        out_shape=(
            jax.ShapeDtypeStruct((t_all, ATT_WIDTH), bf16),
            jax.ShapeDtypeStruct((t_all, KV_WIDTH), bf16),
            jax.ShapeDtypeStruct((KV_WIDTH, t_all), bf16),
            jax.ShapeDtypeStruct((t_all, POOL_WIDTH), f32),
        ),
        grid=(n_tiles,),
        in_specs=[
            pl.BlockSpec((tm, d), lambda i: (i, 0)),
            pl.BlockSpec((1, N_MOD, d), mod_idx),
            pl.BlockSpec((1, 4, d), lambda i: (0, 0, 0)),
            pl.BlockSpec((tm, LANES), tab_idx),
            pl.BlockSpec((tm, LANES), tab_idx),
            pl.BlockSpec(w.shape, lambda i: (0, 0)),
            pl.BlockSpec(wvt.shape, lambda i: (0, 0)),
        ],
        out_specs=(
            pl.BlockSpec((tm, ATT_WIDTH), lambda i: (i, 0)),
            pl.BlockSpec((tm, KV_WIDTH), lambda i: (i, 0)),
            pl.BlockSpec((KV_WIDTH, tm), lambda i: (0, i)),
            pl.BlockSpec((tm, POOL_WIDTH), lambda i: (i, 0)),
        ),
        compiler_params=_params(("parallel",)),
        name="inproj",
    )(x, mod_l, g_l, cos, sin, w, wvt)


def _attn_kernel(sink_ref, q_ref, kp_ref, kc_ref, kn_ref, kx_ref, vp_ref, vc_ref, vn_ref, vx_ref,
                 o_ref, *, nb):
    i = pl.program_id(1)
    nq = ATT_WIDTH // LANES
    nql = nq * BLOCK

    lo_lane = lax.broadcasted_iota(jnp.int32, (1, LANES), 1) < HEAD_DIM
    lo_row = lax.broadcasted_iota(jnp.int32, (KV_WIDTH, 1), 0) < HEAD_DIM
    q4 = jnp.concatenate([q_ref[:, j * LANES:(j + 1) * LANES] for j in range(nq)], axis=0)

    def attend(k_refs, v_refs, biases):
        kcat = jnp.concatenate([r[...] for r in k_refs], axis=0)
        vcat = jnp.concatenate([r[...] for r in v_refs], axis=1)
        nk = kcat.shape[0]
        zk, zv = jnp.zeros_like(kcat), jnp.zeros_like(vcat)
        kk = jnp.concatenate([jnp.where(lo_lane, kcat, zk), jnp.where(lo_lane, zk, kcat)], axis=0)
        vvt = jnp.concatenate([jnp.where(lo_row, vcat, zv), jnp.where(lo_row, zv, vcat)], axis=1)
        orow = lax.broadcasted_iota(jnp.int32, (16, 2 * nk), 0) < 8
        ocol = lax.broadcasted_iota(jnp.int32, (16, 2 * nk), 1) < nk
        vvt = jnp.concatenate([vvt, jnp.where(orow == ocol, 1.0, 0.0).astype(bf16)], axis=0)

        st = lax.dot_general(kk, q4, (((1,), (1,)), ((), ())), preferred_element_type=f32)
        ps, sink_e = [], []
        for half in range(2):
            pieces, off = [], half * nk
            for r, bias in zip(k_refs, biases):
                n = r.shape[0]
                blk = st[off:off + n, :]
                pieces.append(blk if bias is None else blk + bias)
                off += n
            sink = jnp.concatenate(
                [jnp.full((1, BLOCK), sink_ref[2 * j + half] * LOG2_E, f32) for j in range(nq)], axis=1)
            mx = sink
            for blk in pieces:
                mx = jnp.maximum(mx, jnp.max(blk, axis=0, keepdims=True))
            ps.extend(jnp.exp2(blk - mx).astype(bf16) for blk in pieces)
            sink_e.append(jnp.exp2(sink - mx))
        ot = jnp.dot(vvt, jnp.concatenate(ps, axis=0), preferred_element_type=f32)
        inv0 = 1.0 / (ot[KV_WIDTH:KV_WIDTH + 1, :] + sink_e[0])
        inv1 = 1.0 / (ot[KV_WIDTH + 8:KV_WIDTH + 9, :] + sink_e[1])
        o = ot[:KV_WIDTH, :] * jnp.where(lo_row, inv0, inv1)
        for j in range(nq):
            o_ref[:, j * LANES:(j + 1) * LANES] = o[:, j * BLOCK:(j + 1) * BLOCK].T.astype(bf16)

    @pl.when(i < nb)
    def _():
        c = lax.broadcasted_iota(jnp.int32, (BLOCK, BLOCK), 0)
        r = lax.broadcasted_iota(jnp.int32, (BLOCK, BLOCK), 1)
        b_prev = jnp.where(jnp.logical_and(c >= r, i >= 1), 0.0, NEG_INF)
        b_next = jnp.where(jnp.logical_and(c <= r, i <= nb - 2), 0.0, NEG_INF)
        b_prev = jnp.concatenate([b_prev] * nq, axis=1)
        b_next = jnp.concatenate([b_next] * nq, axis=1)
        attend((kp_ref, kc_ref, kn_ref, kx_ref), (vp_ref, vc_ref, vn_ref, vx_ref),
               (b_prev, None, b_next, None))

    @pl.when(i >= nb)
    def _():
        attend((kx_ref,), (vx_ref,), (None,))


def _attention(sinks, q, k, vt, *, n_batch, seq, n_ctx, with_ctx):
    nb = seq // BLOCK
    cb = n_ctx // BLOCK
    n_steps = nb + (cb if with_ctx else 0)
    t_out = n_batch * (seq + (n_ctx if with_ctx else 0))
    ctx_blk0 = (n_batch * seq) // n_ctx

    def q_idx(b, i):
        return (jnp.where(i < nb, b * nb + i, n_batch * nb + b * cb + (i - nb)), 0)

    def prev_blk(b, i):
        return b * nb + jnp.clip(i - 1, 0, nb - 1)

    def cur_blk(b, i):
        return b * nb + jnp.minimum(i, nb - 1)

    def next_blk(b, i):
        return b * nb + jnp.minimum(i + 1, nb - 1)

    k_specs = [
        pl.BlockSpec((BLOCK, KV_WIDTH), lambda b, i: (prev_blk(b, i), 0)),
        pl.BlockSpec((BLOCK, KV_WIDTH), lambda b, i: (cur_blk(b, i), 0)),
        pl.BlockSpec((BLOCK, KV_WIDTH), lambda b, i: (next_blk(b, i), 0)),
        pl.BlockSpec((n_ctx, KV_WIDTH), lambda b, i: (ctx_blk0 + b, 0)),
    ]
    vt_specs = [
        pl.BlockSpec((KV_WIDTH, BLOCK), lambda b, i: (0, prev_blk(b, i))),
        pl.BlockSpec((KV_WIDTH, BLOCK), lambda b, i: (0, cur_blk(b, i))),
        pl.BlockSpec((KV_WIDTH, BLOCK), lambda b, i: (0, next_blk(b, i))),
        pl.BlockSpec((KV_WIDTH, n_ctx), lambda b, i: (0, ctx_blk0 + b)),
    ]
    return pl.pallas_call(
        functools.partial(_attn_kernel, nb=nb),
        out_shape=jax.ShapeDtypeStruct((t_out, ATT_WIDTH), bf16),
        grid=(n_batch, n_steps),
        in_specs=[pl.BlockSpec(memory_space=pltpu.SMEM),
                  pl.BlockSpec((BLOCK, ATT_WIDTH), q_idx)] + k_specs + vt_specs,
        out_specs=pl.BlockSpec((BLOCK, ATT_WIDTH), q_idx),
        compiler_params=_params(("parallel", "parallel")),
        name="attention",
    )(sinks, q, k, k, k, k, vt, vt, vt, vt)


def _pool(p_scr, pc_ref, pooled_scr, row_pos, seq_len, edge_rows, tm):
    h = POOL_HALO

    def ext(lo, n, lanes):
        return p_scr[lo:lo + n, lanes]

    for gi, w in enumerate(POOL_WINDOWS):
        lanes = slice(gi * POOL_GROUP, (gi + 1) * POOL_GROUP)
        if w == 2:
            sums = ext(7, tm, lanes) + ext(8, tm, lanes)
        elif w == 4:
            sums = (ext(6, tm, lanes) + ext(7, tm, lanes)) + (ext(8, tm, lanes) + ext(9, tm, lanes))
        elif w == 8:
            a2 = ext(4, tm + 16, lanes) + ext(5, tm + 16, lanes)
            b2 = a2[0:tm + 8] + a2[2:tm + 10]
            sums = b2[0:tm] + b2[4:tm + 4]
        else:
            a3 = ext(0, tm + 24, lanes) + ext(1, tm + 24, lanes)
            b3 = a3[0:tm + 16] + a3[2:tm + 18]
            c3 = b3[0:tm + 8] + b3[4:tm + 12]
            sums = c3[0:tm] + c3[8:tm + 8]
        pooled_scr[:, lanes] = (sums * (1.0 / w) - pc_ref[:, lanes]).astype(pooled_scr.dtype)

        for g0 in edge_rows:
            pos = row_pos(g0)
            acc = jnp.zeros((h, POOL_GROUP), f32)
            for s in range(-(w // 2), w // 2):
                blk = ext(h + g0 + s, h, lanes)
                if s < 0:
                    blk = jnp.where(pos >= -s, blk, 0.0)
                elif s > 0:
                    blk = jnp.where(pos < seq_len - s, blk, 0.0)
                acc = acc + blk
            cnt = (jnp.minimum(pos + w // 2, seq_len) - jnp.maximum(pos - w // 2, 0)).astype(f32)
            pooled_scr[g0:g0 + h, lanes] = (acc / cnt - pc_ref[g0:g0 + h, lanes]).astype(pooled_scr.dtype)


def _merge_kernel(x_ref, att_ref, pp_ref, pc_ref, pn_ref, mod_ref, g_ref, wgate_ref, watt_ref,
                  wpool_ref, wout_ref, poolw_ref, pscale_ref, *rest,
                  n_lat_tiles, tiles_per_batch, seq, n_ctx, with_router, hf_dtype):
    if with_router:
        wr_ref, br_ref, x1_ref, hf_ref, lg_ref, p_scr, pooled_scr = rest
    else:
        x1_ref, hf_ref, p_scr, pooled_scr = rest
    i = pl.program_id(0)
    tm = x_ref.shape[0]
    d = x_ref.shape[1]
    m = mod_ref[0]
    g = g_ref[0]
    x = x_ref[...]

    h = _rms(x, g[0:1, :])
    h = (h * (1.0 + m[1:2, :]) + m[0:1, :]).astype(bf16)
    gate = 1.0 + jnp.tanh(jnp.dot(h, wgate_ref[...], preferred_element_type=f32))

    p_scr[0:POOL_HALO, :] = pp_ref[...]
    p_scr[POOL_HALO:POOL_HALO + tm, :] = pc_ref[...]
    p_scr[POOL_HALO + tm:2 * POOL_HALO + tm, :] = pn_ref[...]
    p_scr[2 * POOL_HALO + tm:, :] = jnp.zeros((2 * POOL_HALO, POOL_WIDTH), f32)
    is_lat = i < n_lat_tiles
    tile_pos = jnp.where(is_lat, (i % tiles_per_batch) * tm, (i - n_lat_tiles) * tm)
    seq_len = jnp.where(is_lat, seq, n_ctx)
    r8 = lax.broadcasted_iota(jnp.int32, (POOL_HALO, POOL_GROUP), 0)

    def row_pos(g0):
        return jnp.where(is_lat, tile_pos + g0 + r8, (tile_pos + g0 + r8) % n_ctx)

    edge_rows = {0, tm - POOL_HALO}
    for bnd in range(n_ctx, tm, n_ctx):
        edge_rows |= {bnd - POOL_HALO, bnd}
    _pool(p_scr, pc_ref, pooled_scr, row_pos, seq_len, sorted(edge_rows), tm)

    mixed = []
    for gi in range(N_POOL_GROUPS):
        lanes = slice(gi * POOL_GROUP, (gi + 1) * POOL_GROUP)
        mixed.append((jnp.dot(pooled_scr[:, lanes], poolw_ref[gi], preferred_element_type=f32)
                      * pscale_ref[:, lanes]).astype(bf16))
    pool_acc = jnp.dot(jnp.concatenate(mixed, axis=1), wpool_ref[...], preferred_element_type=f32)

    a = jnp.dot(att_ref[...], watt_ref[...], preferred_element_type=f32)
    merged = (gate[:, :d] * a + gate[:, d:] * pool_acc).astype(bf16)
    y = jnp.dot(merged, wout_ref[...], preferred_element_type=f32)
    x1 = x + m[2:3, :] * _rms(y, g[1:2, :])
    x1_ref[...] = x1
    hf = _rms(x1, g[2:3, :]) * (1.0 + m[4:5, :]) + m[3:4, :]
    hf_ref[...] = hf.astype(hf_dtype)
    if with_router:
        lg_ref[...] = jnp.dot(hf.astype(bf16), wr_ref[...], preferred_element_type=f32) + br_ref[...]


def _merge(x, att, p, mod_l, g_l, wgate, watt, wpool, wout, poolw, pscale, router,
           *, n_rows, n_lat_tiles, tiles_per_batch, n_batch, seq, n_ctx, hf_dtype):
    d = x.shape[1]
    tm = TOKEN_TILE
    n_tiles = n_rows // tm
    hb = tm // POOL_HALO
    last_halo = p.shape[0] // POOL_HALO - 1
    with_router = router is not None

    def mod_idx(i):
        return (jnp.minimum(i // tiles_per_batch, n_batch), 0, 0)

    const2 = lambda i: (0, 0)
    in_specs = [
        pl.BlockSpec((tm, d), lambda i: (i, 0)),
        pl.BlockSpec((tm, ATT_WIDTH), lambda i: (i, 0)),
        pl.BlockSpec((POOL_HALO, POOL_WIDTH), lambda i: (jnp.maximum(i * hb - 1, 0), 0)),
        pl.BlockSpec((tm, POOL_WIDTH), lambda i: (i, 0)),
        pl.BlockSpec((POOL_HALO, POOL_WIDTH), lambda i: (jnp.minimum((i + 1) * hb, last_halo), 0)),
        pl.BlockSpec((1, N_MOD, d), mod_idx),
        pl.BlockSpec((1, 4, d), lambda i: (0, 0, 0)),
        pl.BlockSpec(wgate.shape, const2),
        pl.BlockSpec(watt.shape, const2),
        pl.BlockSpec(wpool.shape, const2),
        pl.BlockSpec(wout.shape, const2),
        pl.BlockSpec(poolw.shape, lambda i: (0, 0, 0)),
        pl.BlockSpec(pscale.shape, const2),
    ]
    args = [x, att, p, p, p, mod_l, g_l, wgate, watt, wpool, wout, poolw, pscale]
    out_shape = [jax.ShapeDtypeStruct((n_rows, d), f32), jax.ShapeDtypeStruct((n_rows, d), hf_dtype)]
    out_specs = [pl.BlockSpec((tm, d), lambda i: (i, 0)), pl.BlockSpec((tm, d), lambda i: (i, 0))]
    if with_router:
        wr, br = router
        in_specs += [pl.BlockSpec(wr.shape, const2), pl.BlockSpec(br.shape, const2)]
        args += [wr, br]
        out_shape.append(jax.ShapeDtypeStruct((n_rows, LANES), f32))
        out_specs.append(pl.BlockSpec((tm, LANES), lambda i: (i, 0)))
    return pl.pallas_call(
        functools.partial(_merge_kernel, n_lat_tiles=n_lat_tiles, tiles_per_batch=tiles_per_batch,
                          seq=seq, n_ctx=n_ctx, with_router=with_router, hf_dtype=hf_dtype),
        out_shape=tuple(out_shape),
        grid=(n_tiles,),
        in_specs=in_specs,
        out_specs=tuple(out_specs),
        scratch_shapes=[pltpu.VMEM((tm + 4 * POOL_HALO, POOL_WIDTH), f32),
                        pltpu.VMEM((tm, POOL_WIDTH), bf16)],
        compiler_params=_params(("parallel",)),
        name="merge",
    )(*args)


def _ff_splits(fc):
    edges = list(range(0, fc, FF_SUB)) + [fc]
    return tuple(zip(edges[:-1], edges[1:]))


def _ffn_kernel(te_ref, nv_ref, x_ref, wg_ref, wu_ref, wd_ref, *rest, n_chunks, fuse_residual):
    if fuse_residual:
        xres_ref, mod_ref, g_ref, o_ref, acc_ref = rest
    else:
        o_ref, acc_ref = rest
    i = pl.program_id(0)
    j = pl.program_id(1)

    @pl.when(i < nv_ref[0])
    def _():
        @pl.when(j == 0)
        def _():
            acc_ref[...] = jnp.zeros_like(acc_ref)

        xb = x_ref[...].astype(bf16)
        for c0, c1 in _ff_splits(wg_ref.shape[1]):
            gt = jnp.dot(xb, wg_ref[:, c0:c1], preferred_element_type=f32)
            up = jnp.dot(xb, wu_ref[:, c0:c1], preferred_element_type=f32)
            act = (gt * jax.nn.sigmoid(gt) * up).astype(bf16)
            acc_ref[...] += jnp.dot(act, wd_ref[c0:c1, :], preferred_element_type=f32)

        @pl.when(j == n_chunks - 1)
        def _():
            y = acc_ref[...]
            if fuse_residual:
                m = mod_ref[0]
                y = xres_ref[...] + m[5:6, :] * _rms(y, g_ref[0, 3:4, :])
            o_ref[...] = y

    @pl.when(jnp.logical_and(i >= nv_ref[0], j == n_chunks - 1))
    def _():
        o_ref[...] = jnp.zeros_like(o_ref)


def _ffn(tile_expert, n_valid, x, wg, wu, wd, residual, *, tm, n_chunks):
    n_rows, d = x.shape
    n_tiles = n_rows // tm
    ff = wg.shape[2]
    fc = ff // n_chunks
    fuse = residual is not None

    def row_idx(i, j, te, nv):
        return (jnp.minimum(i, nv[0] - 1), 0)

    def w_in_idx(i, j, te, nv):
        return (te[jnp.minimum(i, nv[0] - 1)], 0, j)

    def w_out_idx(i, j, te, nv):
        return (te[jnp.minimum(i, nv[0] - 1)], j, 0)

    in_specs = [
        pl.BlockSpec((tm, d), row_idx),
        pl.BlockSpec((None, d, fc), w_in_idx),
        pl.BlockSpec((None, d, fc), w_in_idx),
        pl.BlockSpec((None, fc, d), w_out_idx),
    ]
    args = [x, wg, wu, wd]
    if fuse:
        xres, mod_l, g_l, tiles_per_batch, n_batch = residual
        in_specs += [
            pl.BlockSpec((tm, d), row_idx),
            pl.BlockSpec((1, N_MOD, d),
                         lambda i, j, te, nv: (jnp.minimum(i // tiles_per_batch, n_batch), 0, 0)),
            pl.BlockSpec((1, 4, d), lambda i, j, te, nv: (0, 0, 0)),
        ]
        args += [xres, mod_l, g_l]
    return pl.pallas_call(
        functools.partial(_ffn_kernel, n_chunks=n_chunks, fuse_residual=fuse),
        out_shape=jax.ShapeDtypeStruct((n_rows, d), f32),
        grid_spec=pltpu.PrefetchScalarGridSpec(
            num_scalar_prefetch=2,
            grid=(n_tiles, n_chunks),
            in_specs=in_specs,
            out_specs=pl.BlockSpec((tm, d), lambda i, j, te, nv: (i, 0)),
            scratch_shapes=[pltpu.VMEM((tm, d), f32)],
        ),
        compiler_params=_params(("arbitrary", "arbitrary")),
        name="ffn",
    )(tile_expert, n_valid, *args)


def _route_kernel(lg_ref, out_ref, cnt_ref, carry_ref):
    i = pl.program_id(0)
    tm = lg_ref.shape[0]

    @pl.when(i == 0)
    def _():
        carry_ref[...] = jnp.zeros_like(carry_ref)

    lane = lax.broadcasted_iota(jnp.int32, (tm, LANES), 1).astype(f32)
    lg = jnp.where(lane < N_EXPERTS, lg_ref[...], -jnp.inf)
    m1 = jnp.max(lg, axis=1, keepdims=True)
    i1 = jnp.min(jnp.where(lg == m1, lane, float(LANES)), axis=1, keepdims=True)
    lg2 = jnp.where(lane == i1, -jnp.inf, lg)
    m2 = jnp.max(lg2, axis=1, keepdims=True)
    i2 = jnp.min(jnp.where(lg2 == m2, lane, float(LANES)), axis=1, keepdims=True)
    e2 = jnp.exp(m2 - m1)
    w1 = 1.0 / (1.0 + e2)
    w2 = e2 / (1.0 + e2)

    ind = jnp.where(lane == i1, 1.0, jnp.where(lane == i2, 1.0, 0.0))
    rr = lax.broadcasted_iota(jnp.int32, (tm, tm), 0)
    cc = lax.broadcasted_iota(jnp.int32, (tm, tm), 1)
    tri = jnp.where(rr > cc, 1.0, 0.0).astype(bf16)
    rank = jnp.dot(tri, ind.astype(bf16), preferred_element_type=f32) + carry_ref[0:1, :]
    r1 = jnp.sum(jnp.where(lane == i1, rank, 0.0), axis=1, keepdims=True)
    r2 = jnp.sum(jnp.where(lane == i2, rank, 0.0), axis=1, keepdims=True)

    fields = (i1, i2, r1, r2, w1, w2)
    packed = jnp.zeros((tm, LANES), f32)
    for n, fld in enumerate(fields):
        packed = jnp.where(lane == float(n), fld, packed)
    out_ref[...] = packed

    carry_ref[...] = carry_ref[...] + jnp.sum(ind, axis=0, keepdims=True)
    cnt_ref[...] = carry_ref[...]


def _route(logits):
    n_rows = logits.shape[0]
    tm = TOKEN_TILE
    return pl.pallas_call(
        _route_kernel,
        out_shape=(jax.ShapeDtypeStruct((n_rows, LANES), f32), jax.ShapeDtypeStruct((8, LANES), f32)),
        grid=(n_rows // tm,),
        in_specs=[pl.BlockSpec((tm, LANES), lambda i: (i, 0))],
        out_specs=(pl.BlockSpec((tm, LANES), lambda i: (i, 0)), pl.BlockSpec((8, LANES), lambda i: (0, 0))),
        scratch_shapes=[pltpu.VMEM((8, LANES), f32)],
        compiler_params=_params(("arbitrary",)),
        name="route",
    )(logits)


def _row_copy(src_ref, src_row, dst_ref, dst_row, sem):
    return pltpu.make_async_copy(src_ref.at[pl.ds(src_row, 1), :], dst_ref.at[pl.ds(dst_row, 1), :], sem)


def _dispatch_kernel(pos_ref, hf_ref, xs_in_ref, xs_ref, sem):
    del xs_in_ref
    tm = hf_ref.shape[0]

    def start(r, carry):
        _row_copy(hf_ref, r, xs_ref, pos_ref[0, 0, 2 * r], sem).start()
        _row_copy(hf_ref, r, xs_ref, pos_ref[0, 0, 2 * r + 1], sem).start()
        return carry

    lax.fori_loop(0, tm, start, 0, unroll=ROW_DMA_UNROLL)

    def wait(r, carry):
        _row_copy(hf_ref, r, xs_ref, pos_ref[0, 0, 2 * r], sem).wait()
        _row_copy(hf_ref, r, xs_ref, pos_ref[0, 0, 2 * r + 1], sem).wait()
        return carry

    lax.fori_loop(0, tm, wait, 0, unroll=ROW_DMA_UNROLL)


def _dispatch(pos, hf, xs_init):
    n_rows, d = hf.shape
    tm = TOKEN_TILE
    n_tiles = n_rows // tm
    return pl.pallas_call(
        _dispatch_kernel,
        out_shape=jax.ShapeDtypeStruct(xs_init.shape, xs_init.dtype),
        grid=(n_tiles,),
        in_specs=[
            pl.BlockSpec((1, 1, 2 * tm), lambda i: (i, 0, 0), memory_space=pltpu.SMEM),
            pl.BlockSpec((tm, d), lambda i: (i, 0)),
            pl.BlockSpec(memory_space=pl.ANY),
        ],
        out_specs=pl.BlockSpec(memory_space=pl.ANY),
        scratch_shapes=[pltpu.SemaphoreType.DMA(())],
        input_output_aliases={2: 0},
        compiler_params=_params(("arbitrary",)),
        name="dispatch",
    )(pos.reshape(n_tiles, 1, 2 * tm), hf, xs_init)


def _combine_kernel(pos_ref, y_ref, rt_ref, x_ref, mod_ref, g_ref, o_ref, buf0, buf1, sem):
    tm = x_ref.shape[0]

    def start(r, carry):
        _row_copy(y_ref, pos_ref[0, 0, 2 * r], buf0, r, sem).start()
        _row_copy(y_ref, pos_ref[0, 0, 2 * r + 1], buf1, r, sem).start()
        return carry

    lax.fori_loop(0, tm, start, 0, unroll=ROW_DMA_UNROLL)

    def wait(r, carry):
        _row_copy(y_ref, pos_ref[0, 0, 2 * r], buf0, r, sem).wait()
        _row_copy(y_ref, pos_ref[0, 0, 2 * r + 1], buf1, r, sem).wait()
        return carry

    lax.fori_loop(0, tm, wait, 0, unroll=ROW_DMA_UNROLL)

    rt = rt_ref[...]
    y = rt[:, 4:5] * buf0[...] + rt[:, 5:6] * buf1[...]
    m = mod_ref[0]
    o_ref[...] = x_ref[...] + m[5:6, :] * _rms(y, g_ref[0, 3:4, :])


def _combine(pos, y, routed, x, mod_l, g_l, *, tiles_per_batch, n_batch):
    n_rows, d = x.shape
    tm = TOKEN_TILE
    n_tiles = n_rows // tm
    return pl.pallas_call(
        _combine_kernel,
        out_shape=jax.ShapeDtypeStruct((n_rows, d), f32),
        grid=(n_tiles,),
        in_specs=[
            pl.BlockSpec((1, 1, 2 * tm), lambda i: (i, 0, 0), memory_space=pltpu.SMEM),
            pl.BlockSpec(memory_space=pl.ANY),
            pl.BlockSpec((tm, LANES), lambda i: (i, 0)),
            pl.BlockSpec((tm, d), lambda i: (i, 0)),
            pl.BlockSpec((1, N_MOD, d), lambda i: (jnp.minimum(i // tiles_per_batch, n_batch), 0, 0)),
            pl.BlockSpec((1, 4, d), lambda i: (0, 0, 0)),
        ],
        out_specs=pl.BlockSpec((tm, d), lambda i: (i, 0)),
        scratch_shapes=[pltpu.VMEM((tm, d), f32), pltpu.VMEM((tm, d), f32), pltpu.SemaphoreType.DMA(())],
        compiler_params=_params(("arbitrary",)),
        name="combine",
    )(pos.reshape(n_tiles, 1, 2 * tm), y, routed, x, mod_l, g_l)


def _rope_tables(seq):
    half = HEAD_DIM // 4
    freqs = ROPE_BASE ** (-jnp.arange(half, dtype=f32) / half)
    t = jnp.arange(seq)
    row = (t // GRID_W).astype(f32)[:, None] * freqs[None, :]
    col = (t % GRID_W).astype(f32)[:, None] * freqs[None, :]
    cos = jnp.concatenate([jnp.cos(row), jnp.cos(row), jnp.cos(col), jnp.cos(col)], axis=1)
    sin = jnp.concatenate([-jnp.sin(row), jnp.sin(row), -jnp.sin(col), jnp.sin(col)], axis=1)
    reps = LANES // HEAD_DIM
    return jnp.tile(cos, (1, reps)), jnp.tile(sin, (1, reps))


def _head_perm():
    cols = []
    for hq in HEAD_ORDER:
        cols.extend(range(hq * HEAD_DIM, (hq + 1) * HEAD_DIM))
    return jnp.array(cols, dtype=jnp.int32)


def _moe_plan(routed, counts, n_tiles_max):
    te_rows = EXPERT_TILE
    cnt = counts[0, :N_EXPERTS].astype(jnp.int32)
    padded = ((cnt + te_rows - 1) // te_rows) * te_rows
    ends = jnp.cumsum(padded)
    starts = ends - padded
    idx = routed[:, 0:2].astype(jnp.int32)
    rank = routed[:, 2:4].astype(jnp.int32)
    onehot = idx[:, :, None] == jnp.arange(N_EXPERTS, dtype=jnp.int32)[None, None, :]
    pos = rank + jnp.sum(jnp.where(onehot, starts[None, None, :], 0), axis=-1)
    tile_start = jnp.arange(n_tiles_max, dtype=jnp.int32) * te_rows
    tile_expert = jnp.minimum(
        jnp.sum((tile_start[:, None] >= ends[None, :]).astype(jnp.int32), axis=1), N_EXPERTS - 1)
    n_valid = (ends[-1] // te_rows).astype(jnp.int32).reshape(1)
    return pos.reshape(-1), tile_expert.astype(jnp.int32), n_valid


def kernel(x, c, ctx, c_ctx, w_mod, b_mod, norm_g, w_in, sinks, w_att_o, pool_w, pool_scale, w_pool_o,
           w_out, w_ffn_gate, w_ffn_up, w_ffn_down, w_router, b_router, w_exp_gate, w_exp_up,
           w_exp_down):
    n_batch, seq, d = x.shape
    n_ctx = ctx.shape[1]
    depth = w_mod.shape[0]
    tm = TOKEN_TILE
    n_lat = n_batch * seq
    t_all = n_lat + n_batch * n_ctx
    assert seq % tm == 0 and (n_batch * n_ctx) % tm == 0 and n_lat % n_ctx == 0
    assert (n_ctx % tm == 0) or (tm % n_ctx == 0)
    n_lat_tiles = n_lat // tm
    tiles_per_batch = seq // tm
    geom = dict(n_lat_tiles=n_lat_tiles, tiles_per_batch=tiles_per_batch, n_batch=n_batch)

    c8 = jnp.zeros((8, d), f32).at[:n_batch].set(c).at[n_batch].set(c_ctx)
    mod = _modulation(c8, w_mod, b_mod).reshape(depth, 8, N_MOD, d)
    cos, sin = _rope_tables(seq)
    perm = _head_perm()

    v0 = ATT_WIDTH + KV_WIDTH
    qkp_cols = jnp.concatenate([perm, jnp.arange(ATT_WIDTH, v0, dtype=jnp.int32),
                                jnp.arange(v0 + KV_WIDTH, QKVP_WIDTH, dtype=jnp.int32)])
    w_qkp = w_in[:, :, qkp_cols].astype(bf16)
    w_vt = jnp.swapaxes(w_in[:, :, v0:v0 + KV_WIDTH], 1, 2).astype(bf16)
    w_gate = (0.5 * w_in[:, :, QKVP_WIDTH:]).astype(bf16)
    w_att = w_att_o[:, perm, :].astype(bf16)
    w_pool = w_pool_o.astype(bf16)
    w_o = (0.5 * w_out).astype(bf16)
    poolw = pool_w.astype(bf16)
    sinks_p = sinks[:, jnp.array(HEAD_ORDER)]
    w_r = jnp.zeros((w_router.shape[0], d, LANES), bf16).at[:, :, :N_EXPERTS].set(w_router.astype(bf16))
    b_r = jnp.zeros((b_router.shape[0], 1, LANES), f32).at[:, 0, :N_EXPERTS].set(b_router)
    wd_g, wd_u, wd_d = (w.astype(bf16) for w in (w_ffn_gate, w_ffn_up, w_ffn_down))
    we_g, we_u, we_d = (w.astype(bf16) for w in (w_exp_gate, w_exp_up, w_exp_down))

    xa = jnp.concatenate([x.reshape(n_lat, d), ctx.reshape(n_batch * n_ctx, d)], axis=0)

    for l in range(depth):
        last = l == depth - 1
        moe = l % 2 == 1
        n_rows = n_lat if last else t_all
        mod_l = mod[l]
        g_l = norm_g[l:l + 1]

        q, k, vt, p = _inproj(xa, mod_l, g_l, cos, sin, w_qkp[l], w_vt[l], **geom)
        att = _attention(sinks_p[l], q, k, vt, n_batch=n_batch, seq=seq, n_ctx=n_ctx, with_ctx=not last)
        router = (w_r[l // 2], b_r[l // 2]) if moe else None
        outs = _merge(xa, att, p, mod_l, g_l, w_gate[l], w_att[l], w_pool[l], w_o[l], poolw[l],
                      pool_scale[l:l + 1], router, n_rows=n_rows, seq=seq, n_ctx=n_ctx,
                      hf_dtype=f32 if moe else bf16, **geom)
        if not moe:
            x1, hf = outs
            i = l // 2
            n_tiles = n_rows // tm
            xa = _ffn(jnp.zeros((n_tiles,), jnp.int32), jnp.full((1,), n_tiles, jnp.int32), hf,
                      wd_g[i:i + 1], wd_u[i:i + 1], wd_d[i:i + 1],
                      (x1, mod_l, g_l, tiles_per_batch, n_batch), tm=tm, n_chunks=2)
        else:
            x1, hf, logits = outs
            i = l // 2
            routed, counts = _route(logits)
            n_tiles_max = (2 * n_rows) // EXPERT_TILE + N_EXPERTS
            pos, tile_expert, n_valid = _moe_plan(routed, counts, n_tiles_max)
            xs = _dispatch(pos, hf, jnp.zeros((n_tiles_max * EXPERT_TILE, d), f32))
            y = _ffn(tile_expert, n_valid, xs, we_g[i], we_u[i], we_d[i], None,
                     tm=EXPERT_TILE, n_chunks=2)
            xa = _combine(pos, y, routed, x1, mod_l, g_l, tiles_per_batch=tiles_per_batch,
                          n_batch=n_batch)
    return xa[:n_lat].reshape(n_batch, seq, d)
```

```python
import functools

import jax
import jax.numpy as jnp
from jax import lax
from jax.experimental import pallas as pl
from jax.experimental.pallas import tpu as pltpu

GRID_W = 64
HEAD_DIM = 64
N_Q_HEADS = 8
N_KV_HEADS = 2
Q_PER_KV = N_Q_HEADS // N_KV_HEADS
ATT_WIDTH = N_Q_HEADS * HEAD_DIM
KV_WIDTH = N_KV_HEADS * HEAD_DIM
WINDOW = 128
BLOCK = 128
POOL_WINDOWS = (2, 4, 8, 16)
N_POOL_GROUPS = 4
POOL_GROUP = 128
POOL_WIDTH = N_POOL_GROUPS * POOL_GROUP
POOL_HALO = 8
N_MOD = 6
N_EXPERTS = 8
ROPE_BASE = 10000.0
EPS = 1e-6
NEG_INF = -1e30
LOG2_E = 1.4426950408889634
LANES = 128
QKVP_WIDTH = ATT_WIDTH + 2 * KV_WIDTH + POOL_WIDTH

TOKEN_TILE = 512
EXPERT_TILE = 512
FF_SUB = 512
ROW_DMA_UNROLL = 8
VMEM_LIMIT = 56 * 1024 * 1024

HEAD_ORDER = (0, 4, 1, 5, 2, 6, 3, 7)

f32 = jnp.float32
bf16 = jnp.bfloat16


def _params(sem, vmem=VMEM_LIMIT):
    return pltpu.CompilerParams(dimension_semantics=sem, vmem_limit_bytes=vmem)


def _load_token_tiles(ref, tm):
    ns = ref.shape[0] // tm
    return jnp.concatenate([ref[pl.ds(s, tm, stride=ns), :] for s in range(ns)], axis=1)


def _store_token_tiles(ref, val):
    tm = val.shape[0]
    ns = ref.shape[0] // tm
    for s in range(ns):
        ref[pl.ds(s, tm, stride=ns), :] = val[:, s * LANES:(s + 1) * LANES]


def _rms(x, g):
    ms = jnp.mean(x * x, axis=-1, keepdims=True)
    return x * lax.rsqrt(ms + EPS) * g


def _mod_kernel(c_ref, w_ref, b_ref, o_ref):
    c = c_ref[...]
    sc = (c * jax.nn.sigmoid(c)).astype(bf16)
    o_ref[0] = jnp.dot(sc, w_ref[0].astype(bf16), preferred_element_type=f32) + b_ref[0]


def _modulation(c8, w_mod, b_mod):
    depth, d, n = w_mod.shape
    tn = n // 4
    return pl.pallas_call(
        _mod_kernel,
        out_shape=jax.ShapeDtypeStruct((depth, 8, n), f32),
        grid=(depth, n // tn),
        in_specs=[
            pl.BlockSpec((8, d), lambda l, j: (0, 0)),
            pl.BlockSpec((1, d, tn), lambda l, j: (l, 0, j)),
            pl.BlockSpec((1, 1, tn), lambda l, j: (l, 0, j)),
        ],
        out_specs=pl.BlockSpec((1, 8, tn), lambda l, j: (l, 0, j)),
        compiler_params=_params(("arbitrary", "arbitrary")),
        name="modulation",
    )(c8, w_mod, b_mod.reshape(depth, 1, n))


def _inproj_kernel(x_ref, mod_ref, g_ref, cos_ref, sin_ref, w_ref, wvt_ref, q_ref, k_ref, vt_ref, p_ref,
                   *, n_lat_tiles):
    i = pl.program_id(0)
    m = mod_ref[0]
    h = _rms(x_ref[...], g_ref[0, 0:1, :])
    h = (h * (1.0 + m[1:2, :]) + m[0:1, :]).astype(bf16)
    u = jnp.dot(h, w_ref[...], preferred_element_type=f32)
    vt = lax.dot_general(wvt_ref[...], h, (((1,), (1,)), ((), ())), preferred_element_type=f32)
    vt_ref[...] = vt.astype(bf16)

    is_lat = i < n_lat_tiles
    cos = jnp.where(is_lat, cos_ref[...], 1.0)
    sin = jnp.where(is_lat, sin_ref[...], 0.0)
    lane = lax.broadcasted_iota(jnp.int32, (1, LANES), 1)
    first_half = (lane % 32) < 16

    def rope(t):
        partner = jnp.where(first_half, pltpu.roll(t, LANES - 16, axis=1), pltpu.roll(t, 16, axis=1))
        return t * cos + partner * sin

    for jb in range(ATT_WIDTH // LANES):
        blk = rope(u[:, jb * LANES:(jb + 1) * LANES])
        q_ref[:, jb * LANES:(jb + 1) * LANES] = (blk * (LOG2_E * HEAD_DIM ** -0.5)).astype(bf16)
    o = ATT_WIDTH
    k_ref[...] = rope(u[:, o:o + KV_WIDTH]).astype(bf16)
    o += KV_WIDTH
    p_ref[...] = u[:, o:o + POOL_WIDTH]


def _inproj(x, mod_l, g_l, cos, sin, layer, w, wvt, *, n_lat_tiles, tiles_per_batch, n_batch):
    t_all, d = x.shape
    tm = TOKEN_TILE
    n_tiles = t_all // tm

    def mod_idx(i):
        return (jnp.minimum(i // tiles_per_batch, n_batch), 0, 0)

    def tab_idx(i):
        return (jnp.where(i < n_lat_tiles, i % tiles_per_batch, 0), 0)

    return pl.pallas_call(
        functools.partial(_inproj_kernel, n_lat_tiles=n_lat_tiles),
        out_shape=(
            jax.ShapeDtypeStruct((t_all, ATT_WIDTH), bf16),
            jax.ShapeDtypeStruct((t_all, KV_WIDTH), bf16),
            jax.ShapeDtypeStruct((KV_WIDTH, t_all), bf16),
            jax.ShapeDtypeStruct((t_all, POOL_WIDTH), f32),
        ),
        grid=(n_tiles,),
        in_specs=[
            pl.BlockSpec((tm, d), lambda i: (i, 0)),
            pl.BlockSpec((1, N_MOD, d), mod_idx),
            pl.BlockSpec((1, 4, d), lambda i: (0, 0, 0)),
            pl.BlockSpec((tm, LANES), tab_idx),
            pl.BlockSpec((tm, LANES), tab_idx),
            pl.BlockSpec((None,) + w.shape[1:], lambda i: (layer, 0, 0)),
            pl.BlockSpec((None,) + wvt.shape[1:], lambda i: (layer, 0, 0)),
        ],
        out_specs=(
            pl.BlockSpec((tm, ATT_WIDTH), lambda i: (i, 0)),
            pl.BlockSpec((tm, KV_WIDTH), lambda i: (i, 0)),
            pl.BlockSpec((KV_WIDTH, tm), lambda i: (0, i)),
            pl.BlockSpec((tm, POOL_WIDTH), lambda i: (i, 0)),
        ),
        compiler_params=_params(("parallel",)),
        name="inproj",
    )(x, mod_l, g_l, cos, sin, w, wvt)


def _attn_kernel(sink_ref, q_ref, kp_ref, kc_ref, kn_ref, kx_ref, vp_ref, vc_ref, vn_ref, vx_ref,
                 o_ref, *, nb):
    i = pl.program_id(1)
    nq = ATT_WIDTH // LANES
    nql = nq * BLOCK

    lo_lane = lax.broadcasted_iota(jnp.int32, (1, LANES), 1) < HEAD_DIM
    lo_row = lax.broadcasted_iota(jnp.int32, (KV_WIDTH, 1), 0) < HEAD_DIM
    q4 = jnp.concatenate([q_ref[:, j * LANES:(j + 1) * LANES] for j in range(nq)], axis=0)

    def attend(k_refs, v_refs, biases):
        kcat = jnp.concatenate([r[...] for r in k_refs], axis=0)
        vcat = jnp.concatenate([r[...] for r in v_refs], axis=1)
        nk = kcat.shape[0]
        zk, zv = jnp.zeros_like(kcat), jnp.zeros_like(vcat)
        kk = jnp.concatenate([jnp.where(lo_lane, kcat, zk), jnp.where(lo_lane, zk, kcat)], axis=0)
        vvt = jnp.concatenate([jnp.where(lo_row, vcat, zv), jnp.where(lo_row, zv, vcat)], axis=1)
        orow = lax.broadcasted_iota(jnp.int32, (16, 2 * nk), 0) < 8
        ocol = lax.broadcasted_iota(jnp.int32, (16, 2 * nk), 1) < nk
        vvt = jnp.concatenate([vvt, jnp.where(orow == ocol, 1.0, 0.0).astype(bf16)], axis=0)

        st = lax.dot_general(kk, q4, (((1,), (1,)), ((), ())), preferred_element_type=f32)
        ps, sink_e = [], []
        for half in range(2):
            pieces, off = [], half * nk
            for r, bias in zip(k_refs, biases):
                n = r.shape[0]
                blk = st[off:off + n, :]
                pieces.append(blk if bias is None else blk + bias)
                off += n
            sink = jnp.concatenate(
                [jnp.full((1, BLOCK), sink_ref[2 * j + half] * LOG2_E, f32) for j in range(nq)], axis=1)
            mx = sink
            for blk in pieces:
                mx = jnp.maximum(mx, jnp.max(blk, axis=0, keepdims=True))
            ps.extend(jnp.exp2(blk - mx).astype(bf16) for blk in pieces)
            sink_e.append(jnp.exp2(sink - mx))
        ot = jnp.dot(vvt, jnp.concatenate(ps, axis=0), preferred_element_type=f32)
        inv0 = 1.0 / (ot[KV_WIDTH:KV_WIDTH + 1, :] + sink_e[0])
        inv1 = 1.0 / (ot[KV_WIDTH + 8:KV_WIDTH + 9, :] + sink_e[1])
        o = ot[:KV_WIDTH, :] * jnp.where(lo_row, inv0, inv1)
        for j in range(nq):
            o_ref[:, j * LANES:(j + 1) * LANES] = o[:, j * BLOCK:(j + 1) * BLOCK].T.astype(bf16)

    @pl.when(i < nb)
    def _():
        c = lax.broadcasted_iota(jnp.int32, (BLOCK, BLOCK), 0)
        r = lax.broadcasted_iota(jnp.int32, (BLOCK, BLOCK), 1)
        b_prev = jnp.where(jnp.logical_and(c >= r, i >= 1), 0.0, NEG_INF)
        b_next = jnp.where(jnp.logical_and(c <= r, i <= nb - 2), 0.0, NEG_INF)
        b_prev = jnp.concatenate([b_prev] * nq, axis=1)
        b_next = jnp.concatenate([b_next] * nq, axis=1)
        attend((kp_ref, kc_ref, kn_ref, kx_ref), (vp_ref, vc_ref, vn_ref, vx_ref),
               (b_prev, None, b_next, None))

    @pl.when(i >= nb)
    def _():
        attend((kx_ref,), (vx_ref,), (None,))


def _attention(sinks, q, k, vt, *, n_batch, seq, n_ctx, with_ctx):
    nb = seq // BLOCK
    cb = n_ctx // BLOCK
    n_steps = nb + (cb if with_ctx else 0)
    t_out = n_batch * (seq + (n_ctx if with_ctx else 0))
    ctx_blk0 = (n_batch * seq) // n_ctx

    def q_idx(b, i):
        return (jnp.where(i < nb, b * nb + i, n_batch * nb + b * cb + (i - nb)), 0)

    def prev_blk(b, i):
        return b * nb + jnp.clip(i - 1, 0, nb - 1)

    def cur_blk(b, i):
        return b * nb + jnp.minimum(i, nb - 1)

    def next_blk(b, i):
        return b * nb + jnp.minimum(i + 1, nb - 1)

    k_specs = [
        pl.BlockSpec((BLOCK, KV_WIDTH), lambda b, i: (prev_blk(b, i), 0)),
        pl.BlockSpec((BLOCK, KV_WIDTH), lambda b, i: (cur_blk(b, i), 0)),
        pl.BlockSpec((BLOCK, KV_WIDTH), lambda b, i: (next_blk(b, i), 0)),
        pl.BlockSpec((n_ctx, KV_WIDTH), lambda b, i: (ctx_blk0 + b, 0)),
    ]
    vt_specs = [
        pl.BlockSpec((KV_WIDTH, BLOCK), lambda b, i: (0, prev_blk(b, i))),
        pl.BlockSpec((KV_WIDTH, BLOCK), lambda b, i: (0, cur_blk(b, i))),
        pl.BlockSpec((KV_WIDTH, BLOCK), lambda b, i: (0, next_blk(b, i))),
        pl.BlockSpec((KV_WIDTH, n_ctx), lambda b, i: (0, ctx_blk0 + b)),
    ]
    return pl.pallas_call(
        functools.partial(_attn_kernel, nb=nb),
        out_shape=jax.ShapeDtypeStruct((t_out, ATT_WIDTH), bf16),
        grid=(n_batch, n_steps),
        in_specs=[pl.BlockSpec(memory_space=pltpu.SMEM),
                  pl.BlockSpec((BLOCK, ATT_WIDTH), q_idx)] + k_specs + vt_specs,
        out_specs=pl.BlockSpec((BLOCK, ATT_WIDTH), q_idx),
        compiler_params=_params(("parallel", "parallel")),
        name="attention",
    )(sinks, q, k, k, k, k, vt, vt, vt, vt)


def _pool(p_scr, pc_ref, pooled_scr, row_pos, seq_len, edge_rows, tm):
    h = POOL_HALO

    def ext(lo, n, lanes):
        return p_scr[lo:lo + n, lanes]

    for gi, w in enumerate(POOL_WINDOWS):
        lanes = slice(gi * POOL_GROUP, (gi + 1) * POOL_GROUP)
        if w == 2:
            sums = ext(7, tm, lanes) + ext(8, tm, lanes)
        elif w == 4:
            sums = (ext(6, tm, lanes) + ext(7, tm, lanes)) + (ext(8, tm, lanes) + ext(9, tm, lanes))
        elif w == 8:
            a2 = ext(4, tm + 16, lanes) + ext(5, tm + 16, lanes)
            b2 = a2[0:tm + 8] + a2[2:tm + 10]
            sums = b2[0:tm] + b2[4:tm + 4]
        else:
            a3 = ext(0, tm + 24, lanes) + ext(1, tm + 24, lanes)
            b3 = a3[0:tm + 16] + a3[2:tm + 18]
            c3 = b3[0:tm + 8] + b3[4:tm + 12]
            sums = c3[0:tm] + c3[8:tm + 8]
        pooled_scr[:, lanes] = (sums * (1.0 / w) - pc_ref[:, lanes]).astype(pooled_scr.dtype)

        for g0 in edge_rows:
            pos = row_pos(g0)
            acc = jnp.zeros((h, POOL_GROUP), f32)
            for s in range(-(w // 2), w // 2):
                blk = ext(h + g0 + s, h, lanes)
                if s < 0:
                    blk = jnp.where(pos >= -s, blk, 0.0)
                elif s > 0:
                    blk = jnp.where(pos < seq_len - s, blk, 0.0)
                acc = acc + blk
            cnt = (jnp.minimum(pos + w // 2, seq_len) - jnp.maximum(pos - w // 2, 0)).astype(f32)
            pooled_scr[g0:g0 + h, lanes] = (acc / cnt - pc_ref[g0:g0 + h, lanes]).astype(pooled_scr.dtype)


def _merge_kernel(x_ref, att_ref, pp_ref, pc_ref, pn_ref, mod_ref, g_ref, wgate_ref, watt_ref,
                  wpool_ref, wout_ref, poolw_ref, pscale_ref, *rest,
                  n_lat_tiles, tiles_per_batch, seq, n_ctx, with_router):
    if with_router:
        wr_ref, br_ref, x1_ref, hf_ref, lg_ref, p_scr, pooled_scr = rest
    else:
        x1_ref, hf_ref, p_scr, pooled_scr = rest
    i = pl.program_id(0)
    tm = x_ref.shape[0]
    d = x_ref.shape[1]
    m = mod_ref[0]
    g = g_ref[0]
    x = x_ref[...]

    h = _rms(x, g[0:1, :])
    h = (h * (1.0 + m[1:2, :]) + m[0:1, :]).astype(bf16)
    gate = 1.0 + jnp.tanh(jnp.dot(h, wgate_ref[...], preferred_element_type=f32))

    p_scr[0:POOL_HALO, :] = pp_ref[...]
    p_scr[POOL_HALO:POOL_HALO + tm, :] = pc_ref[...]
    p_scr[POOL_HALO + tm:2 * POOL_HALO + tm, :] = pn_ref[...]
    p_scr[2 * POOL_HALO + tm:, :] = jnp.zeros((2 * POOL_HALO, POOL_WIDTH), f32)
    is_lat = i < n_lat_tiles
    tile_pos = jnp.where(is_lat, (i % tiles_per_batch) * tm, (i - n_lat_tiles) * tm)
    seq_len = jnp.where(is_lat, seq, n_ctx)
    r8 = lax.broadcasted_iota(jnp.int32, (POOL_HALO, POOL_GROUP), 0)

    def row_pos(g0):
        return jnp.where(is_lat, tile_pos + g0 + r8, (tile_pos + g0 + r8) % n_ctx)

    edge_rows = {0, tm - POOL_HALO}
    for bnd in range(n_ctx, tm, n_ctx):
        edge_rows |= {bnd - POOL_HALO, bnd}
    _pool(p_scr, pc_ref, pooled_scr, row_pos, seq_len, sorted(edge_rows), tm)

    mixed = []
    for gi in range(N_POOL_GROUPS):
        lanes = slice(gi * POOL_GROUP, (gi + 1) * POOL_GROUP)
        mixed.append((jnp.dot(pooled_scr[:, lanes], poolw_ref[gi], preferred_element_type=f32)
                      * pscale_ref[:, lanes]).astype(bf16))
    pool_acc = jnp.dot(jnp.concatenate(mixed, axis=1), wpool_ref[...], preferred_element_type=f32)

    a = jnp.dot(att_ref[...], watt_ref[...], preferred_element_type=f32)
    merged = (gate[:, :d] * a + gate[:, d:] * pool_acc).astype(bf16)
    y = jnp.dot(merged, wout_ref[...], preferred_element_type=f32)
    x1 = x + m[2:3, :] * _rms(y, g[1:2, :])
    x1_ref[...] = x1
    hf = _rms(x1, g[2:3, :]) * (1.0 + m[4:5, :]) + m[3:4, :]
    if with_router:
        _store_token_tiles(hf_ref, hf)
        lg_ref[...] = jnp.dot(hf.astype(bf16), wr_ref[...], preferred_element_type=f32) + br_ref[...]
    else:
        hf_ref[...] = hf.astype(bf16)


def _merge(x, att, p, mod_l, g_l, layer, wgate, watt, wpool, wout, poolw, pscale, router,
           *, n_rows, n_lat_tiles, tiles_per_batch, n_batch, seq, n_ctx):
    d = x.shape[1]
    tm = TOKEN_TILE
    n_tiles = n_rows // tm
    hb = tm // POOL_HALO
    last_halo = p.shape[0] // POOL_HALO - 1
    with_router = router is not None

    def mod_idx(i):
        return (jnp.minimum(i // tiles_per_batch, n_batch), 0, 0)

    def layer_spec(w, idx=None):
        idx = layer if idx is None else idx
        zeros = (0,) * (w.ndim - 1)
        return pl.BlockSpec((None,) + w.shape[1:], lambda i: (idx,) + zeros)

    in_specs = [
        pl.BlockSpec((tm, d), lambda i: (i, 0)),
        pl.BlockSpec((tm, ATT_WIDTH), lambda i: (i, 0)),
        pl.BlockSpec((POOL_HALO, POOL_WIDTH), lambda i: (jnp.maximum(i * hb - 1, 0), 0)),
        pl.BlockSpec((tm, POOL_WIDTH), lambda i: (i, 0)),
        pl.BlockSpec((POOL_HALO, POOL_WIDTH), lambda i: (jnp.minimum((i + 1) * hb, last_halo), 0)),
        pl.BlockSpec((1, N_MOD, d), mod_idx),
        pl.BlockSpec((1, 4, d), lambda i: (0, 0, 0)),
        layer_spec(wgate), layer_spec(watt), layer_spec(wpool), layer_spec(wout), layer_spec(poolw),
        layer_spec(pscale),
    ]
    args = [x, att, p, p, p, mod_l, g_l, wgate, watt, wpool, wout, poolw, pscale]
    out_shape = [jax.ShapeDtypeStruct((n_rows, d), f32)]
    out_specs = [pl.BlockSpec((tm, d), lambda i: (i, 0))]
    if with_router:
        wr, br = router
        in_specs += [layer_spec(wr, layer // 2), layer_spec(br, layer // 2)]
        args += [wr, br]
        ns = d // LANES
        out_shape += [jax.ShapeDtypeStruct((n_rows * ns, LANES), f32),
                      jax.ShapeDtypeStruct((n_rows, LANES), f32)]
        out_specs += [pl.BlockSpec((tm * ns, LANES), lambda i: (i, 0)),
                      pl.BlockSpec((tm, LANES), lambda i: (i, 0))]
    else:
        out_shape.append(jax.ShapeDtypeStruct((n_rows, d), bf16))
        out_specs.append(pl.BlockSpec((tm, d), lambda i: (i, 0)))
    return pl.pallas_call(
        functools.partial(_merge_kernel, n_lat_tiles=n_lat_tiles, tiles_per_batch=tiles_per_batch,
                          seq=seq, n_ctx=n_ctx, with_router=with_router),
        out_shape=tuple(out_shape),
        grid=(n_tiles,),
        in_specs=in_specs,
        out_specs=tuple(out_specs),
        scratch_shapes=[pltpu.VMEM((tm + 4 * POOL_HALO, POOL_WIDTH), f32),
                        pltpu.VMEM((tm, POOL_WIDTH), bf16)],
        compiler_params=_params(("parallel",)),
        name="merge",
    )(*args)


def _ff_splits(fc):
    edges = list(range(0, fc, FF_SUB)) + [fc]
    return tuple(zip(edges[:-1], edges[1:]))


def _ffn_kernel(te_ref, nv_ref, x_ref, wg_ref, wu_ref, wd_ref, *rest, n_chunks, fuse_residual, tiled_rows):
    if fuse_residual:
        xres_ref, mod_ref, g_ref, o_ref, acc_ref = rest
    else:
        o_ref, acc_ref = rest
    i = pl.program_id(0)
    j = pl.program_id(1)

    @pl.when(i < nv_ref[0])
    def _():
        @pl.when(j == 0)
        def _():
            acc_ref[...] = jnp.zeros_like(acc_ref)

        tm = acc_ref.shape[0]
        xb = (_load_token_tiles(x_ref, tm) if tiled_rows else x_ref[...]).astype(bf16)
        for c0, c1 in _ff_splits(wg_ref.shape[1]):
            gt = jnp.dot(xb, wg_ref[:, c0:c1], preferred_element_type=f32)
            up = jnp.dot(xb, wu_ref[:, c0:c1], preferred_element_type=f32)
            act = (gt * jax.nn.sigmoid(gt) * up).astype(bf16)
            acc_ref[...] += jnp.dot(act, wd_ref[c0:c1, :], preferred_element_type=f32)

        @pl.when(j == n_chunks - 1)
        def _():
            y = acc_ref[...]
            if fuse_residual:
                m = mod_ref[0]
                y = xres_ref[...] + m[5:6, :] * _rms(y, g_ref[0, 3:4, :])
            if tiled_rows:
                _store_token_tiles(o_ref, y)
            else:
                o_ref[...] = y

    @pl.when(jnp.logical_and(i >= nv_ref[0], j == n_chunks - 1))
    def _():
        o_ref[...] = jnp.zeros_like(o_ref)


def _ffn(tile_expert, n_valid, x, wg, wu, wd, residual, *, tm, n_chunks, tiled_rows):
    d = wg.shape[1]
    ns = d // LANES if tiled_rows else 1
    n_rows = x.shape[0] // ns
    n_tiles = n_rows // tm
    ff = wg.shape[2]
    fc = ff // n_chunks
    fuse = residual is not None
    row_block = (tm * ns, LANES) if tiled_rows else (tm, d)

    def row_idx(i, j, te, nv):
        return (jnp.minimum(i, nv[0] - 1), 0)

    def w_in_idx(i, j, te, nv):
        return (te[jnp.minimum(i, nv[0] - 1)], 0, j)

    def w_out_idx(i, j, te, nv):
        return (te[jnp.minimum(i, nv[0] - 1)], j, 0)

    in_specs = [
        pl.BlockSpec(row_block, row_idx),
        pl.BlockSpec((None, d, fc), w_in_idx),
        pl.BlockSpec((None, d, fc), w_in_idx),
        pl.BlockSpec((None, fc, d), w_out_idx),
    ]
    args = [x, wg, wu, wd]
    if fuse:
        xres, mod_l, g_l, tiles_per_batch, n_batch = residual
        in_specs += [
            pl.BlockSpec((tm, d), row_idx),
            pl.BlockSpec((1, N_MOD, d),
                         lambda i, j, te, nv: (jnp.minimum(i // tiles_per_batch, n_batch), 0, 0)),
            pl.BlockSpec((1, 4, d), lambda i, j, te, nv: (0, 0, 0)),
        ]
        args += [xres, mod_l, g_l]
    return pl.pallas_call(
        functools.partial(_ffn_kernel, n_chunks=n_chunks, fuse_residual=fuse, tiled_rows=tiled_rows),
        out_shape=jax.ShapeDtypeStruct((n_rows * ns, row_block[1]), f32),
        grid_spec=pltpu.PrefetchScalarGridSpec(
            num_scalar_prefetch=2,
            grid=(n_tiles, n_chunks),
            in_specs=in_specs,
            out_specs=pl.BlockSpec(row_block, lambda i, j, te, nv: (i, 0)),
            scratch_shapes=[pltpu.VMEM((tm, d), f32)],
        ),
        compiler_params=_params(("arbitrary", "arbitrary")),
        name="ffn",
    )(tile_expert, n_valid, *args)


def _route_kernel(lg_ref, out_ref, cnt_ref, carry_ref):
    i = pl.program_id(0)
    tm = lg_ref.shape[0]

    @pl.when(i == 0)
    def _():
        carry_ref[...] = jnp.zeros_like(carry_ref)

    lane = lax.broadcasted_iota(jnp.int32, (tm, LANES), 1).astype(f32)
    lg = jnp.where(lane < N_EXPERTS, lg_ref[...], -jnp.inf)
    m1 = jnp.max(lg, axis=1, keepdims=True)
    i1 = jnp.min(jnp.where(lg == m1, lane, float(LANES)), axis=1, keepdims=True)
    lg2 = jnp.where(lane == i1, -jnp.inf, lg)
    m2 = jnp.max(lg2, axis=1, keepdims=True)
    i2 = jnp.min(jnp.where(lg2 == m2, lane, float(LANES)), axis=1, keepdims=True)
    e2 = jnp.exp(m2 - m1)
    w1 = 1.0 / (1.0 + e2)
    w2 = e2 / (1.0 + e2)

    ind = jnp.where(lane == i1, 1.0, jnp.where(lane == i2, 1.0, 0.0))
    rr = lax.broadcasted_iota(jnp.int32, (tm, tm), 0)
    cc = lax.broadcasted_iota(jnp.int32, (tm, tm), 1)
    tri = jnp.where(rr > cc, 1.0, 0.0).astype(bf16)
    rank = jnp.dot(tri, ind.astype(bf16), preferred_element_type=f32) + carry_ref[0:1, :]
    r1 = jnp.sum(jnp.where(lane == i1, rank, 0.0), axis=1, keepdims=True)
    r2 = jnp.sum(jnp.where(lane == i2, rank, 0.0), axis=1, keepdims=True)

    fields = (i1, i2, r1, r2, w1, w2)
    packed = jnp.zeros((tm, LANES), f32)
    for n, fld in enumerate(fields):
        packed = jnp.where(lane == float(n), fld, packed)
    out_ref[...] = packed

    carry_ref[...] = carry_ref[...] + jnp.sum(ind, axis=0, keepdims=True)
    cnt_ref[...] = carry_ref[...]


def _route(logits):
    n_rows = logits.shape[0]
    tm = TOKEN_TILE
    return pl.pallas_call(
        _route_kernel,
        out_shape=(jax.ShapeDtypeStruct((n_rows, LANES), f32), jax.ShapeDtypeStruct((8, LANES), f32)),
        grid=(n_rows // tm,),
        in_specs=[pl.BlockSpec((tm, LANES), lambda i: (i, 0))],
        out_specs=(pl.BlockSpec((tm, LANES), lambda i: (i, 0)), pl.BlockSpec((8, LANES), lambda i: (0, 0))),
        scratch_shapes=[pltpu.VMEM((8, LANES), f32)],
        compiler_params=_params(("arbitrary",)),
        name="route",
    )(logits)


def _token_copy(src_ref, src_tok, dst_ref, dst_tok, sem, ns):
    src = src_ref.at[pl.ds(pl.multiple_of(src_tok * ns, ns), ns), :]
    dst = dst_ref.at[pl.ds(pl.multiple_of(dst_tok * ns, ns), ns), :]
    return pltpu.make_async_copy(src, dst, sem)


def _dispatch_kernel(pad_ref, pos_ref, hf_ref, xs_ref, zero_ref, sem, *, tm, pad_rows):
    i = pl.program_id(0)
    ns = hf_ref.shape[0] // tm

    @pl.when(i == 0)
    def _():
        zero_ref[...] = jnp.zeros_like(zero_ref)

        def zero_fill(e):
            dst = xs_ref.at[pl.ds(pl.multiple_of(pad_ref[e] * ns, ns), pad_rows * ns), :]
            return pltpu.make_async_copy(zero_ref, dst, sem)

        def zero_tail(t):
            rows = pad_rows * ns
            return pltpu.make_async_copy(zero_ref, xs_ref.at[pl.ds(xs_ref.shape[0] - (t + 1) * rows, rows), :], sem)

        for cp in [zero_tail(t) for t in range(N_EXPERTS + 1)] + [zero_fill(e) for e in range(N_EXPERTS)]:
            cp.start()
            cp.wait()

    def start(r, carry):
        _token_copy(hf_ref, r, xs_ref, pos_ref[0, 0, 2 * r], sem, ns).start()
        _token_copy(hf_ref, r, xs_ref, pos_ref[0, 0, 2 * r + 1], sem, ns).start()
        return carry

    lax.fori_loop(0, tm, start, 0, unroll=ROW_DMA_UNROLL)

    def wait(r, carry):
        _token_copy(hf_ref, r, xs_ref, pos_ref[0, 0, 2 * r], sem, ns).wait()
        _token_copy(hf_ref, r, xs_ref, pos_ref[0, 0, 2 * r + 1], sem, ns).wait()
        return carry

    lax.fori_loop(0, tm, wait, 0, unroll=ROW_DMA_UNROLL)


def _dispatch(pad_start, pos, hf, n_sorted_rows, ns):
    tm = TOKEN_TILE
    n_tiles = hf.shape[0] // (tm * ns)
    return pl.pallas_call(
        functools.partial(_dispatch_kernel, tm=tm, pad_rows=EXPERT_TILE),
        out_shape=jax.ShapeDtypeStruct((n_sorted_rows * ns, LANES), f32),
        grid_spec=pltpu.PrefetchScalarGridSpec(
            num_scalar_prefetch=1,
            grid=(n_tiles,),
            in_specs=[
                pl.BlockSpec((1, 1, 2 * tm), lambda i, pad: (i, 0, 0), memory_space=pltpu.SMEM),
                pl.BlockSpec((tm * ns, LANES), lambda i, pad: (i, 0)),
            ],
            out_specs=pl.BlockSpec(memory_space=pl.ANY),
            scratch_shapes=[pltpu.VMEM((EXPERT_TILE * ns, LANES), f32), pltpu.SemaphoreType.DMA(())],
        ),
        compiler_params=_params(("arbitrary",)),
        name="dispatch",
    )(pad_start, pos.reshape(n_tiles, 1, 2 * tm), hf)


def _combine_kernel(pos_ref, pos_next_ref, y_ref, rt_ref, x_ref, mod_ref, g_ref, o_ref, buf0, buf1, sems):
    i = pl.program_id(0)
    tm = x_ref.shape[0]
    ns = buf0.shape[1] // tm
    slot = i % 2

    def copies(p_ref, sl, r):
        return (_token_copy(y_ref, p_ref[0, 0, 2 * r], buf0.at[sl], r, sems.at[sl], ns),
                _token_copy(y_ref, p_ref[0, 0, 2 * r + 1], buf1.at[sl], r, sems.at[sl], ns))

    def issue(p_ref, sl):
        def start(r, carry):
            for cp in copies(p_ref, sl, r):
                cp.start()
            return carry

        lax.fori_loop(0, tm, start, 0, unroll=ROW_DMA_UNROLL)

    @pl.when(i == 0)
    def _():
        issue(pos_ref, slot)

    @pl.when(i + 1 < pl.num_programs(0))
    def _():
        issue(pos_next_ref, 1 - slot)

    def wait(r, carry):
        for cp in copies(pos_ref, slot, r):
            cp.wait()
        return carry

    lax.fori_loop(0, tm, wait, 0, unroll=ROW_DMA_UNROLL)

    rt = rt_ref[...]
    y = (rt[:, 4:5] * _load_token_tiles(buf0.at[slot], tm)
         + rt[:, 5:6] * _load_token_tiles(buf1.at[slot], tm))
    m = mod_ref[0]
    o_ref[...] = x_ref[...] + m[5:6, :] * _rms(y, g_ref[0, 3:4, :])


def _combine(pos, y, routed, x, mod_l, g_l, *, tiles_per_batch, n_batch):
    n_rows, d = x.shape
    tm = TOKEN_TILE
    n_tiles = n_rows // tm
    ns = d // LANES
    pos3 = pos.reshape(n_tiles, 1, 2 * tm)
    return pl.pallas_call(
        _combine_kernel,
        out_shape=jax.ShapeDtypeStruct((n_rows, d), f32),
        grid=(n_tiles,),
        in_specs=[
            pl.BlockSpec((1, 1, 2 * tm), lambda i: (i, 0, 0), memory_space=pltpu.SMEM),
            pl.BlockSpec((1, 1, 2 * tm), lambda i: (jnp.minimum(i + 1, n_tiles - 1), 0, 0),
                         memory_space=pltpu.SMEM),
            pl.BlockSpec(memory_space=pl.ANY),
            pl.BlockSpec((tm, LANES), lambda i: (i, 0)),
            pl.BlockSpec((tm, d), lambda i: (i, 0)),
            pl.BlockSpec((1, N_MOD, d), lambda i: (jnp.minimum(i // tiles_per_batch, n_batch), 0, 0)),
            pl.BlockSpec((1, 4, d), lambda i: (0, 0, 0)),
        ],
        out_specs=pl.BlockSpec((tm, d), lambda i: (i, 0)),
        scratch_shapes=[pltpu.VMEM((2, tm * ns, LANES), f32), pltpu.VMEM((2, tm * ns, LANES), f32),
                        pltpu.SemaphoreType.DMA((2,))],
        compiler_params=_params(("arbitrary",)),
        name="combine",
    )(pos3, pos3, y, routed, x, mod_l, g_l)


def _rope_tables(seq):
    half = HEAD_DIM // 4
    freqs = ROPE_BASE ** (-jnp.arange(half, dtype=f32) / half)
    t = jnp.arange(seq)
    row = (t // GRID_W).astype(f32)[:, None] * freqs[None, :]
    col = (t % GRID_W).astype(f32)[:, None] * freqs[None, :]
    cos = jnp.concatenate([jnp.cos(row), jnp.cos(row), jnp.cos(col), jnp.cos(col)], axis=1)
    sin = jnp.concatenate([-jnp.sin(row), jnp.sin(row), -jnp.sin(col), jnp.sin(col)], axis=1)
    reps = LANES // HEAD_DIM
    return jnp.tile(cos, (1, reps)), jnp.tile(sin, (1, reps))


def _head_perm():
    cols = []
    for hq in HEAD_ORDER:
        cols.extend(range(hq * HEAD_DIM, (hq + 1) * HEAD_DIM))
    return jnp.array(cols, dtype=jnp.int32)


def _moe_plan(routed, counts, n_tiles_max):
    te_rows = EXPERT_TILE
    cnt = counts[0, :N_EXPERTS].astype(jnp.int32)
    padded = ((cnt + te_rows - 1) // te_rows) * te_rows
    ends = jnp.cumsum(padded)
    starts = ends - padded
    idx = routed[:, 0:2].astype(jnp.int32)
    rank = routed[:, 2:4].astype(jnp.int32)
    onehot = idx[:, :, None] == jnp.arange(N_EXPERTS, dtype=jnp.int32)[None, None, :]
    pos = rank + jnp.sum(jnp.where(onehot, starts[None, None, :], 0), axis=-1)
    tile_start = jnp.arange(n_tiles_max, dtype=jnp.int32) * te_rows
    tile_expert = jnp.minimum(
        jnp.sum((tile_start[:, None] >= ends[None, :]).astype(jnp.int32), axis=1), N_EXPERTS - 1)
    n_valid = (ends[-1] // te_rows).astype(jnp.int32).reshape(1)
    pad_start = (starts + cnt).astype(jnp.int32)
    return pos.reshape(-1), tile_expert.astype(jnp.int32), n_valid, pad_start


def kernel(x, c, ctx, c_ctx, w_mod, b_mod, norm_g, w_in, sinks, w_att_o, pool_w, pool_scale, w_pool_o,
           w_out, w_ffn_gate, w_ffn_up, w_ffn_down, w_router, b_router, w_exp_gate, w_exp_up,
           w_exp_down):
    n_batch, seq, d = x.shape
    n_ctx = ctx.shape[1]
    depth = w_mod.shape[0]
    tm = TOKEN_TILE
    n_lat = n_batch * seq
    t_all = n_lat + n_batch * n_ctx
    assert seq % tm == 0 and (n_batch * n_ctx) % tm == 0 and n_lat % n_ctx == 0
    assert (n_ctx % tm == 0) or (tm % n_ctx == 0)
    n_lat_tiles = n_lat // tm
    tiles_per_batch = seq // tm
    geom = dict(n_lat_tiles=n_lat_tiles, tiles_per_batch=tiles_per_batch, n_batch=n_batch)

    c8 = jnp.zeros((8, d), f32).at[:n_batch].set(c).at[n_batch].set(c_ctx)
    mod = _modulation(c8, w_mod, b_mod).reshape(depth, 8, N_MOD, d)
    cos, sin = _rope_tables(seq)
    perm = _head_perm()

    v0 = ATT_WIDTH + KV_WIDTH
    qkp_cols = jnp.concatenate([perm, jnp.arange(ATT_WIDTH, v0, dtype=jnp.int32),
                                jnp.arange(v0 + KV_WIDTH, QKVP_WIDTH, dtype=jnp.int32)])
    w_qkp = w_in[:, :, qkp_cols].astype(bf16)
    w_vt = jnp.swapaxes(w_in[:, :, v0:v0 + KV_WIDTH], 1, 2).astype(bf16)
    w_gate = (0.5 * w_in[:, :, QKVP_WIDTH:]).astype(bf16)
    w_att = w_att_o[:, perm, :].astype(bf16)
    w_pool = w_pool_o.astype(bf16)
    w_o = (0.5 * w_out).astype(bf16)
    poolw = pool_w.astype(bf16)
    sinks_p = sinks[:, jnp.array(HEAD_ORDER)]
    w_r = jnp.zeros((w_router.shape[0], d, LANES), bf16).at[:, :, :N_EXPERTS].set(w_router.astype(bf16))
    b_r = jnp.zeros((b_router.shape[0], 1, LANES), f32).at[:, 0, :N_EXPERTS].set(b_router)
    wd_g, wd_u, wd_d = (w.astype(bf16) for w in (w_ffn_gate, w_ffn_up, w_ffn_down))
    we_g, we_u, we_d = (w.astype(bf16) for w in (w_exp_gate, w_exp_up, w_exp_down))

    xa = jnp.concatenate([x.reshape(n_lat, d), ctx.reshape(n_batch * n_ctx, d)], axis=0)

    n_e = w_exp_gate.shape[1]
    we_g, we_u, we_d = (w.reshape((-1,) + w.shape[2:]) for w in (we_g, we_u, we_d))
    pscale = pool_scale[:, None, :]
    ns = d // LANES

    for l in range(depth):
        last = l == depth - 1
        moe = l % 2 == 1
        n_rows = n_lat if last else t_all
        mod_l = mod[l]
        g_l = norm_g[l:l + 1]

        q, k, vt, p = _inproj(xa, mod_l, g_l, cos, sin, l, w_qkp, w_vt, **geom)
        att = _attention(sinks_p[l], q, k, vt, n_batch=n_batch, seq=seq, n_ctx=n_ctx, with_ctx=not last)
        router = (w_r, b_r) if moe else None
        outs = _merge(xa, att, p, mod_l, g_l, l, w_gate, w_att, w_pool, w_o, poolw, pscale, router,
                      n_rows=n_rows, seq=seq, n_ctx=n_ctx, **geom)
        i = l // 2
        if not moe:
            x1, hf = outs
            n_tiles = n_rows // tm
            xa = _ffn(jnp.full((n_tiles,), i, jnp.int32), jnp.full((1,), n_tiles, jnp.int32), hf,
                      wd_g, wd_u, wd_d, (x1, mod_l, g_l, tiles_per_batch, n_batch),
                      tm=tm, n_chunks=1, tiled_rows=False)
        else:
            x1, hf, logits = outs
            routed, counts = _route(logits)
            n_tiles_max = (2 * n_rows) // EXPERT_TILE + N_EXPERTS
            pos, tile_expert, n_valid, pad_start = _moe_plan(routed, counts, n_tiles_max)
            xs = _dispatch(pad_start, pos, hf, (n_tiles_max + 1) * EXPERT_TILE, ns)
            y = _ffn(tile_expert + i * n_e, n_valid, xs, we_g, we_u, we_d, None,
                     tm=EXPERT_TILE, n_chunks=2, tiled_rows=True)
            xa = _combine(pos, y, routed, x1, mod_l, g_l, tiles_per_batch=tiles_per_batch,
                          n_batch=n_batch)
    return xa[:n_lat].reshape(n_batch, seq, d)
```

```python
import functools

import jax
import jax.numpy as jnp
from jax import lax
from jax.experimental import pallas as pl
from jax.experimental.pallas import tpu as pltpu

GRID_W = 64
HEAD_DIM = 64
N_Q_HEADS = 8
N_KV_HEADS = 2
Q_PER_KV = N_Q_HEADS // N_KV_HEADS
ATT_WIDTH = N_Q_HEADS * HEAD_DIM
KV_WIDTH = N_KV_HEADS * HEAD_DIM
WINDOW = 128
BLOCK = 128
POOL_WINDOWS = (2, 4, 8, 16)
N_POOL_GROUPS = 4
POOL_GROUP = 128
POOL_WIDTH = N_POOL_GROUPS * POOL_GROUP
POOL_HALO = 8
N_MOD = 6
N_EXPERTS = 8
ROPE_BASE = 10000.0
EPS = 1e-6
NEG_INF = -1e30
LOG2_E = 1.4426950408889634
LANES = 128
QKVP_WIDTH = ATT_WIDTH + 2 * KV_WIDTH + POOL_WIDTH

TOKEN_TILE = 512
EXPERT_TILE = 512
FF_SUB = 512
Q_GROUP = 4
ROW_DMA_UNROLL = 8
VMEM_LIMIT = 56 * 1024 * 1024

HEAD_ORDER = (0, 4, 1, 5, 2, 6, 3, 7)

f32 = jnp.float32
bf16 = jnp.bfloat16


def _params(sem, vmem=VMEM_LIMIT):
    return pltpu.CompilerParams(dimension_semantics=sem, vmem_limit_bytes=vmem)


def _load_token_tiles(ref, tm):
    ns = ref.shape[0] // tm
    return jnp.concatenate([ref[pl.ds(s, tm, stride=ns), :] for s in range(ns)], axis=1)


def _store_token_tiles(ref, val):
    tm = val.shape[0]
    ns = ref.shape[0] // tm
    for s in range(ns):
        ref[pl.ds(s, tm, stride=ns), :] = val[:, s * LANES:(s + 1) * LANES]


def _rms(x, g):
    ms = jnp.mean(x * x, axis=-1, keepdims=True)
    return x * lax.rsqrt(ms + EPS) * g


def _mod_kernel(c_ref, w_ref, b_ref, o_ref):
    c = c_ref[...]
    sc = (c * jax.nn.sigmoid(c)).astype(bf16)
    o_ref[0] = jnp.dot(sc, w_ref[0].astype(bf16), preferred_element_type=f32) + b_ref[0]


def _modulation(c8, w_mod, b_mod):
    depth, d, n = w_mod.shape
    tn = n // 4
    return pl.pallas_call(
        _mod_kernel,
        out_shape=jax.ShapeDtypeStruct((depth, 8, n), f32),
        grid=(depth, n // tn),
        in_specs=[
            pl.BlockSpec((8, d), lambda l, j: (0, 0)),
            pl.BlockSpec((1, d, tn), lambda l, j: (l, 0, j)),
            pl.BlockSpec((1, 1, tn), lambda l, j: (l, 0, j)),
        ],
        out_specs=pl.BlockSpec((1, 8, tn), lambda l, j: (l, 0, j)),
        compiler_params=_params(("arbitrary", "arbitrary")),
        name="modulation",
    )(c8, w_mod, b_mod.reshape(depth, 1, n))


def _inproj_kernel(x_ref, mod_ref, g_ref, cos_ref, sin_ref, w_ref, wvt_ref, q_ref, k_ref, vt_ref, p_ref,
                   *, n_lat_tiles):
    i = pl.program_id(0)
    m = mod_ref[0]
    h = _rms(x_ref[...], g_ref[0, 0:1, :])
    h = (h * (1.0 + m[1:2, :]) + m[0:1, :]).astype(bf16)
    u = jnp.dot(h, w_ref[...], preferred_element_type=f32)
    vt = lax.dot_general(wvt_ref[...], h, (((1,), (1,)), ((), ())), preferred_element_type=f32)
    vt_ref[...] = vt.astype(bf16)

    is_lat = i < n_lat_tiles
    cos = jnp.where(is_lat, cos_ref[...], 1.0)
    sin = jnp.where(is_lat, sin_ref[...], 0.0)
    lane = lax.broadcasted_iota(jnp.int32, (1, LANES), 1)
    first_half = (lane % 32) < 16

    def rope(t):
        partner = jnp.where(first_half, pltpu.roll(t, LANES - 16, axis=1), pltpu.roll(t, 16, axis=1))
        return t * cos + partner * sin

    for jb in range(ATT_WIDTH // LANES):
        blk = rope(u[:, jb * LANES:(jb + 1) * LANES])
        q_ref[:, jb * LANES:(jb + 1) * LANES] = (blk * (LOG2_E * HEAD_DIM ** -0.5)).astype(bf16)
    o = ATT_WIDTH
    k_ref[...] = rope(u[:, o:o + KV_WIDTH]).astype(bf16)
    o += KV_WIDTH
    p_ref[...] = u[:, o:o + POOL_WIDTH]


def _inproj(x, mod_l, g_l, cos, sin, layer, w, wvt, *, n_lat_tiles, tiles_per_batch, n_batch):
    t_all, d = x.shape
    tm = TOKEN_TILE
    n_tiles = t_all // tm

    def mod_idx(i):
        return (jnp.minimum(i // tiles_per_batch, n_batch), 0, 0)

    def tab_idx(i):
        return (jnp.where(i < n_lat_tiles, i % tiles_per_batch, 0), 0)

    return pl.pallas_call(
        functools.partial(_inproj_kernel, n_lat_tiles=n_lat_tiles),
        out_shape=(
            jax.ShapeDtypeStruct((t_all, ATT_WIDTH), bf16),
            jax.ShapeDtypeStruct((t_all, KV_WIDTH), bf16),
            jax.ShapeDtypeStruct((KV_WIDTH, t_all), bf16),
            jax.ShapeDtypeStruct((t_all, POOL_WIDTH), f32),
        ),
        grid=(n_tiles,),
        in_specs=[
            pl.BlockSpec((tm, d), lambda i: (i, 0)),
            pl.BlockSpec((1, N_MOD, d), mod_idx),
            pl.BlockSpec((1, 4, d), lambda i: (0, 0, 0)),
            pl.BlockSpec((tm, LANES), tab_idx),
            pl.BlockSpec((tm, LANES), tab_idx),
            pl.BlockSpec((None,) + w.shape[1:], lambda i: (layer, 0, 0)),
            pl.BlockSpec((None,) + wvt.shape[1:], lambda i: (layer, 0, 0)),
        ],
        out_specs=(
            pl.BlockSpec((tm, ATT_WIDTH), lambda i: (i, 0)),
            pl.BlockSpec((tm, KV_WIDTH), lambda i: (i, 0)),
            pl.BlockSpec((KV_WIDTH, tm), lambda i: (0, i)),
            pl.BlockSpec((tm, POOL_WIDTH), lambda i: (i, 0)),
        ),
        compiler_params=_params(("parallel",)),
        name="inproj",
    )(x, mod_l, g_l, cos, sin, w, wvt)


def _attn_kernel(sink_ref, q_ref, kp_ref, kc_ref, kn_ref, kx_ref, vp_ref, vc_ref, vn_ref, vx_ref,
                 o_ref, *, nb):
    i = pl.program_id(1)
    nq = ATT_WIDTH // LANES

    lo_lane = lax.broadcasted_iota(jnp.int32, (1, LANES), 1) < HEAD_DIM
    lo_row = lax.broadcasted_iota(jnp.int32, (KV_WIDTH, 1), 0) < HEAD_DIM

    def attend(k_refs, v_refs, biases):
        kcat = jnp.concatenate([r[...] for r in k_refs], axis=0)
        vcat = jnp.concatenate([r[...] for r in v_refs], axis=1)
        nk = kcat.shape[0]
        zk, zv = jnp.zeros_like(kcat), jnp.zeros_like(vcat)
        kk = jnp.concatenate([jnp.where(lo_lane, kcat, zk), jnp.where(lo_lane, zk, kcat)], axis=0)
        vvt = jnp.concatenate([jnp.where(lo_row, vcat, zv), jnp.where(lo_row, zv, vcat)], axis=1)
        orow = lax.broadcasted_iota(jnp.int32, (16, 2 * nk), 0) < 8
        ocol = lax.broadcasted_iota(jnp.int32, (16, 2 * nk), 1) < nk
        vvt = jnp.concatenate([vvt, jnp.where(orow == ocol, 1.0, 0.0).astype(bf16)], axis=0)

        for j0 in range(0, nq, Q_GROUP):
            js = range(j0, j0 + Q_GROUP)
            qg = jnp.concatenate([q_ref[:, j * LANES:(j + 1) * LANES] for j in js], axis=0)
            st = lax.dot_general(kk, qg, (((1,), (1,)), ((), ())), preferred_element_type=f32)
            ps, sink_e = [], []
            for half in range(2):
                pieces, off = [], half * nk
                for r, bias in zip(k_refs, biases):
                    n = r.shape[0]
                    blk = st[off:off + n, :]
                    pieces.append(blk if bias is None else blk + bias)
                    off += n
                sink = jnp.concatenate(
                    [jnp.full((1, BLOCK), sink_ref[2 * j + half] * LOG2_E, f32) for j in js], axis=1)
                mx = sink
                for blk in pieces:
                    mx = jnp.maximum(mx, jnp.max(blk, axis=0, keepdims=True))
                ps.extend(jnp.exp2(blk - mx).astype(bf16) for blk in pieces)
                sink_e.append(jnp.exp2(sink - mx))
            ot = jnp.dot(vvt, jnp.concatenate(ps, axis=0), preferred_element_type=f32)
            inv0 = 1.0 / (ot[KV_WIDTH:KV_WIDTH + 1, :] + sink_e[0])
            inv1 = 1.0 / (ot[KV_WIDTH + 8:KV_WIDTH + 9, :] + sink_e[1])
            o = ot[:KV_WIDTH, :] * jnp.where(lo_row, inv0, inv1)
            for n, j in enumerate(js):
                o_ref[:, j * LANES:(j + 1) * LANES] = o[:, n * BLOCK:(n + 1) * BLOCK].T.astype(bf16)

    @pl.when(i < nb)
    def _():
        c = lax.broadcasted_iota(jnp.int32, (BLOCK, BLOCK), 0)
        r = lax.broadcasted_iota(jnp.int32, (BLOCK, BLOCK), 1)
        b_prev = jnp.where(jnp.logical_and(c >= r, i >= 1), 0.0, NEG_INF)
        b_next = jnp.where(jnp.logical_and(c <= r, i <= nb - 2), 0.0, NEG_INF)
        b_prev = jnp.concatenate([b_prev] * Q_GROUP, axis=1)
        b_next = jnp.concatenate([b_next] * Q_GROUP, axis=1)
        attend((kp_ref, kc_ref, kn_ref, kx_ref), (vp_ref, vc_ref, vn_ref, vx_ref),
               (b_prev, None, b_next, None))

    @pl.when(i >= nb)
    def _():
        attend((kx_ref,), (vx_ref,), (None,))


def _attention(sinks, q, k, vt, *, n_batch, seq, n_ctx, with_ctx):
    nb = seq // BLOCK
    cb = n_ctx // BLOCK
    n_steps = nb + (cb if with_ctx else 0)
    t_out = n_batch * (seq + (n_ctx if with_ctx else 0))
    ctx_blk0 = (n_batch * seq) // n_ctx

    def q_idx(b, i):
        return (jnp.where(i < nb, b * nb + i, n_batch * nb + b * cb + (i - nb)), 0)

    def prev_blk(b, i):
        return b * nb + jnp.clip(i - 1, 0, nb - 1)

    def cur_blk(b, i):
        return b * nb + jnp.minimum(i, nb - 1)

    def next_blk(b, i):
        return b * nb + jnp.minimum(i + 1, nb - 1)

    k_specs = [
        pl.BlockSpec((BLOCK, KV_WIDTH), lambda b, i: (prev_blk(b, i), 0)),
        pl.BlockSpec((BLOCK, KV_WIDTH), lambda b, i: (cur_blk(b, i), 0)),
        pl.BlockSpec((BLOCK, KV_WIDTH), lambda b, i: (next_blk(b, i), 0)),
        pl.BlockSpec((n_ctx, KV_WIDTH), lambda b, i: (ctx_blk0 + b, 0)),
    ]
    vt_specs = [
        pl.BlockSpec((KV_WIDTH, BLOCK), lambda b, i: (0, prev_blk(b, i))),
        pl.BlockSpec((KV_WIDTH, BLOCK), lambda b, i: (0, cur_blk(b, i))),
        pl.BlockSpec((KV_WIDTH, BLOCK), lambda b, i: (0, next_blk(b, i))),
        pl.BlockSpec((KV_WIDTH, n_ctx), lambda b, i: (0, ctx_blk0 + b)),
    ]
    return pl.pallas_call(
        functools.partial(_attn_kernel, nb=nb),
        out_shape=jax.ShapeDtypeStruct((t_out, ATT_WIDTH), bf16),
        grid=(n_batch, n_steps),
        in_specs=[pl.BlockSpec(memory_space=pltpu.SMEM),
                  pl.BlockSpec((BLOCK, ATT_WIDTH), q_idx)] + k_specs + vt_specs,
        out_specs=pl.BlockSpec((BLOCK, ATT_WIDTH), q_idx),
        compiler_params=_params(("parallel", "parallel")),
        name="attention",
    )(sinks, q, k, k, k, k, vt, vt, vt, vt)


def _pool(p_scr, pc_ref, pooled_scr, row_pos, seq_len, edge_rows, tm):
    h = POOL_HALO

    def ext(lo, n, lanes):
        return p_scr[lo:lo + n, lanes]

    for gi, w in enumerate(POOL_WINDOWS):
        lanes = slice(gi * POOL_GROUP, (gi + 1) * POOL_GROUP)
        if w == 2:
            sums = ext(7, tm, lanes) + ext(8, tm, lanes)
        elif w == 4:
            sums = (ext(6, tm, lanes) + ext(7, tm, lanes)) + (ext(8, tm, lanes) + ext(9, tm, lanes))
        elif w == 8:
            a2 = ext(4, tm + 16, lanes) + ext(5, tm + 16, lanes)
            b2 = a2[0:tm + 8] + a2[2:tm + 10]
            sums = b2[0:tm] + b2[4:tm + 4]
        else:
            a3 = ext(0, tm + 24, lanes) + ext(1, tm + 24, lanes)
            b3 = a3[0:tm + 16] + a3[2:tm + 18]
            c3 = b3[0:tm + 8] + b3[4:tm + 12]
            sums = c3[0:tm] + c3[8:tm + 8]
        pooled_scr[:, lanes] = (sums * (1.0 / w) - pc_ref[:, lanes]).astype(pooled_scr.dtype)

        for g0 in edge_rows:
            pos = row_pos(g0)
            acc = jnp.zeros((h, POOL_GROUP), f32)
            for s in range(-(w // 2), w // 2):
                blk = ext(h + g0 + s, h, lanes)
                if s < 0:
                    blk = jnp.where(pos >= -s, blk, 0.0)
                elif s > 0:
                    blk = jnp.where(pos < seq_len - s, blk, 0.0)
                acc = acc + blk
            cnt = (jnp.minimum(pos + w // 2, seq_len) - jnp.maximum(pos - w // 2, 0)).astype(f32)
            pooled_scr[g0:g0 + h, lanes] = (acc / cnt - pc_ref[g0:g0 + h, lanes]).astype(pooled_scr.dtype)


def _merge_kernel(x_ref, att_ref, pp_ref, pc_ref, pn_ref, mod_ref, g_ref, wgate_ref, watt_ref,
                  wpool_ref, wout_ref, poolw_ref, pscale_ref, *rest,
                  n_lat_tiles, tiles_per_batch, seq, n_ctx, with_router):
    if with_router:
        wr_ref, br_ref, x1_ref, hf_ref, lg_ref, p_scr, pooled_scr = rest
    else:
        x1_ref, hf_ref, p_scr, pooled_scr = rest
    i = pl.program_id(0)
    tm = x_ref.shape[0]
    d = x_ref.shape[1]
    m = mod_ref[0]
    g = g_ref[0]
    x = x_ref[...]

    h = _rms(x, g[0:1, :])
    h = (h * (1.0 + m[1:2, :]) + m[0:1, :]).astype(bf16)
    gate = 1.0 + jnp.tanh(jnp.dot(h, wgate_ref[...], preferred_element_type=f32))

    p_scr[0:POOL_HALO, :] = pp_ref[...]
    p_scr[POOL_HALO:POOL_HALO + tm, :] = pc_ref[...]
    p_scr[POOL_HALO + tm:2 * POOL_HALO + tm, :] = pn_ref[...]
    p_scr[2 * POOL_HALO + tm:, :] = jnp.zeros((2 * POOL_HALO, POOL_WIDTH), f32)
    is_lat = i < n_lat_tiles
    tile_pos = jnp.where(is_lat, (i % tiles_per_batch) * tm, (i - n_lat_tiles) * tm)
    seq_len = jnp.where(is_lat, seq, n_ctx)
    r8 = lax.broadcasted_iota(jnp.int32, (POOL_HALO, POOL_GROUP), 0)

    def row_pos(g0):
        return jnp.where(is_lat, tile_pos + g0 + r8, (tile_pos + g0 + r8) % n_ctx)

    edge_rows = {0, tm - POOL_HALO}
    for bnd in range(n_ctx, tm, n_ctx):
        edge_rows |= {bnd - POOL_HALO, bnd}
    _pool(p_scr, pc_ref, pooled_scr, row_pos, seq_len, sorted(edge_rows), tm)

    mixed = []
    for gi in range(N_POOL_GROUPS):
        lanes = slice(gi * POOL_GROUP, (gi + 1) * POOL_GROUP)
        mixed.append((jnp.dot(pooled_scr[:, lanes], poolw_ref[gi], preferred_element_type=f32)
                      * pscale_ref[:, lanes]).astype(bf16))
    pool_acc = jnp.dot(jnp.concatenate(mixed, axis=1), wpool_ref[...], preferred_element_type=f32)

    a = jnp.dot(att_ref[...], watt_ref[...], preferred_element_type=f32)
    merged = (gate[:, :d] * a + gate[:, d:] * pool_acc).astype(bf16)
    y = jnp.dot(merged, wout_ref[...], preferred_element_type=f32)
    x1 = x + m[2:3, :] * _rms(y, g[1:2, :])
    x1_ref[...] = x1
    hf = _rms(x1, g[2:3, :]) * (1.0 + m[4:5, :]) + m[3:4, :]
    if with_router:
        _store_token_tiles(hf_ref, hf)
        lg_ref[...] = jnp.dot(hf.astype(bf16), wr_ref[...], preferred_element_type=f32) + br_ref[...]
    else:
        hf_ref[...] = hf.astype(bf16)


def _merge(x, att, p, mod_l, g_l, layer, wgate, watt, wpool, wout, poolw, pscale, router,
           *, n_rows, n_lat_tiles, tiles_per_batch, n_batch, seq, n_ctx):
    d = x.shape[1]
    tm = TOKEN_TILE
    n_tiles = n_rows // tm
    hb = tm // POOL_HALO
    last_halo = p.shape[0] // POOL_HALO - 1
    with_router = router is not None

    def mod_idx(i):
        return (jnp.minimum(i // tiles_per_batch, n_batch), 0, 0)

    def layer_spec(w, idx=None):
        idx = layer if idx is None else idx
        zeros = (0,) * (w.ndim - 1)
        return pl.BlockSpec((None,) + w.shape[1:], lambda i: (idx,) + zeros)

    in_specs = [
        pl.BlockSpec((tm, d), lambda i: (i, 0)),
        pl.BlockSpec((tm, ATT_WIDTH), lambda i: (i, 0)),
        pl.BlockSpec((POOL_HALO, POOL_WIDTH), lambda i: (jnp.maximum(i * hb - 1, 0), 0)),
        pl.BlockSpec((tm, POOL_WIDTH), lambda i: (i, 0)),
        pl.BlockSpec((POOL_HALO, POOL_WIDTH), lambda i: (jnp.minimum((i + 1) * hb, last_halo), 0)),
        pl.BlockSpec((1, N_MOD, d), mod_idx),
        pl.BlockSpec((1, 4, d), lambda i: (0, 0, 0)),
        layer_spec(wgate), layer_spec(watt), layer_spec(wpool), layer_spec(wout), layer_spec(poolw),
        layer_spec(pscale),
    ]
    args = [x, att, p, p, p, mod_l, g_l, wgate, watt, wpool, wout, poolw, pscale]
    out_shape = [jax.ShapeDtypeStruct((n_rows, d), f32)]
    out_specs = [pl.BlockSpec((tm, d), lambda i: (i, 0))]
    if with_router:
        wr, br = router
        in_specs += [layer_spec(wr, layer // 2), layer_spec(br, layer // 2)]
        args += [wr, br]
        ns = d // LANES
        out_shape += [jax.ShapeDtypeStruct((n_rows * ns, LANES), f32),
                      jax.ShapeDtypeStruct((n_rows, LANES), f32)]
        out_specs += [pl.BlockSpec((tm * ns, LANES), lambda i: (i, 0)),
                      pl.BlockSpec((tm, LANES), lambda i: (i, 0))]
    else:
        out_shape.append(jax.ShapeDtypeStruct((n_rows, d), bf16))
        out_specs.append(pl.BlockSpec((tm, d), lambda i: (i, 0)))
    return pl.pallas_call(
        functools.partial(_merge_kernel, n_lat_tiles=n_lat_tiles, tiles_per_batch=tiles_per_batch,
                          seq=seq, n_ctx=n_ctx, with_router=with_router),
        out_shape=tuple(out_shape),
        grid=(n_tiles,),
        in_specs=in_specs,
        out_specs=tuple(out_specs),
        scratch_shapes=[pltpu.VMEM((tm + 4 * POOL_HALO, POOL_WIDTH), f32),
                        pltpu.VMEM((tm, POOL_WIDTH), bf16)],
        compiler_params=_params(("parallel",)),
        name="merge",
    )(*args)


def _ff_splits(fc):
    edges = list(range(0, fc, FF_SUB)) + [fc]
    return tuple(zip(edges[:-1], edges[1:]))


def _ffn_kernel(te_ref, nv_ref, x_ref, wg_ref, wu_ref, wd_ref, *rest, n_chunks, fuse_residual, tiled_rows):
    if fuse_residual:
        xres_ref, mod_ref, g_ref, o_ref, acc_ref = rest
    else:
        o_ref, acc_ref = rest
    i = pl.program_id(0)
    j = pl.program_id(1)

    @pl.when(i < nv_ref[0])
    def _():
        @pl.when(j == 0)
        def _():
            acc_ref[...] = jnp.zeros_like(acc_ref)

        tm = acc_ref.shape[0]
        xb = (_load_token_tiles(x_ref, tm) if tiled_rows else x_ref[...]).astype(bf16)
        for c0, c1 in _ff_splits(wg_ref.shape[1]):
            gt = jnp.dot(xb, wg_ref[:, c0:c1], preferred_element_type=f32)
            up = jnp.dot(xb, wu_ref[:, c0:c1], preferred_element_type=f32)
            act = (gt * jax.nn.sigmoid(gt) * up).astype(bf16)
            acc_ref[...] += jnp.dot(act, wd_ref[c0:c1, :], preferred_element_type=f32)

        @pl.when(j == n_chunks - 1)
        def _():
            y = acc_ref[...]
            if fuse_residual:
                m = mod_ref[0]
                y = xres_ref[...] + m[5:6, :] * _rms(y, g_ref[0, 3:4, :])
            if tiled_rows:
                _store_token_tiles(o_ref, y)
            else:
                o_ref[...] = y

    @pl.when(jnp.logical_and(i >= nv_ref[0], j == n_chunks - 1))
    def _():
        o_ref[...] = jnp.zeros_like(o_ref)


def _ffn(tile_expert, n_valid, x, wg, wu, wd, residual, *, tm, n_chunks, tiled_rows):
    d = wg.shape[1]
    ns = d // LANES if tiled_rows else 1
    n_rows = x.shape[0] // ns
    n_tiles = n_rows // tm
    ff = wg.shape[2]
    fc = ff // n_chunks
    fuse = residual is not None
    row_block = (tm * ns, LANES) if tiled_rows else (tm, d)

    def row_idx(i, j, te, nv):
        return (jnp.minimum(i, nv[0] - 1), 0)

    def w_in_idx(i, j, te, nv):
        return (te[jnp.minimum(i, nv[0] - 1)], 0, j)

    def w_out_idx(i, j, te, nv):
        return (te[jnp.minimum(i, nv[0] - 1)], j, 0)

    in_specs = [
        pl.BlockSpec(row_block, row_idx),
        pl.BlockSpec((None, d, fc), w_in_idx),
        pl.BlockSpec((None, d, fc), w_in_idx),
        pl.BlockSpec((None, fc, d), w_out_idx),
    ]
    args = [x, wg, wu, wd]
    if fuse:
        xres, mod_l, g_l, tiles_per_batch, n_batch = residual
        in_specs += [
            pl.BlockSpec((tm, d), row_idx),
            pl.BlockSpec((1, N_MOD, d),
                         lambda i, j, te, nv: (jnp.minimum(i // tiles_per_batch, n_batch), 0, 0)),
            pl.BlockSpec((1, 4, d), lambda i, j, te, nv: (0, 0, 0)),
        ]
        args += [xres, mod_l, g_l]
    return pl.pallas_call(
        functools.partial(_ffn_kernel, n_chunks=n_chunks, fuse_residual=fuse, tiled_rows=tiled_rows),
        out_shape=jax.ShapeDtypeStruct((n_rows * ns, row_block[1]), f32),
        grid_spec=pltpu.PrefetchScalarGridSpec(
            num_scalar_prefetch=2,
            grid=(n_tiles, n_chunks),
            in_specs=in_specs,
            out_specs=pl.BlockSpec(row_block, lambda i, j, te, nv: (i, 0)),
            scratch_shapes=[pltpu.VMEM((tm, d), f32)],
        ),
        compiler_params=_params(("arbitrary", "arbitrary")),
        name="ffn",
    )(tile_expert, n_valid, *args)


def _route_kernel(lg_ref, out_ref, cnt_ref, carry_ref):
    i = pl.program_id(0)
    tm = lg_ref.shape[0]

    @pl.when(i == 0)
    def _():
        carry_ref[...] = jnp.zeros_like(carry_ref)

    lane = lax.broadcasted_iota(jnp.int32, (tm, LANES), 1).astype(f32)
    lg = jnp.where(lane < N_EXPERTS, lg_ref[...], -jnp.inf)
    m1 = jnp.max(lg, axis=1, keepdims=True)
    i1 = jnp.min(jnp.where(lg == m1, lane, float(LANES)), axis=1, keepdims=True)
    lg2 = jnp.where(lane == i1, -jnp.inf, lg)
    m2 = jnp.max(lg2, axis=1, keepdims=True)
    i2 = jnp.min(jnp.where(lg2 == m2, lane, float(LANES)), axis=1, keepdims=True)
    e2 = jnp.exp(m2 - m1)
    w1 = 1.0 / (1.0 + e2)
    w2 = e2 / (1.0 + e2)

    ind = jnp.where(lane == i1, 1.0, jnp.where(lane == i2, 1.0, 0.0))
    rr = lax.broadcasted_iota(jnp.int32, (tm, tm), 0)
    cc = lax.broadcasted_iota(jnp.int32, (tm, tm), 1)
    tri = jnp.where(rr > cc, 1.0, 0.0).astype(bf16)
    rank = jnp.dot(tri, ind.astype(bf16), preferred_element_type=f32) + carry_ref[0:1, :]
    r1 = jnp.sum(jnp.where(lane == i1, rank, 0.0), axis=1, keepdims=True)
    r2 = jnp.sum(jnp.where(lane == i2, rank, 0.0), axis=1, keepdims=True)

    fields = (i1, i2, r1, r2, w1, w2)
    packed = jnp.zeros((tm, LANES), f32)
    for n, fld in enumerate(fields):
        packed = jnp.where(lane == float(n), fld, packed)
    out_ref[...] = packed

    carry_ref[...] = carry_ref[...] + jnp.sum(ind, axis=0, keepdims=True)
    cnt_ref[...] = carry_ref[...]


def _route(logits):
    n_rows = logits.shape[0]
    tm = TOKEN_TILE
    return pl.pallas_call(
        _route_kernel,
        out_shape=(jax.ShapeDtypeStruct((n_rows, LANES), f32), jax.ShapeDtypeStruct((8, LANES), f32)),
        grid=(n_rows // tm,),
        in_specs=[pl.BlockSpec((tm, LANES), lambda i: (i, 0))],
        out_specs=(pl.BlockSpec((tm, LANES), lambda i: (i, 0)), pl.BlockSpec((8, LANES), lambda i: (0, 0))),
        scratch_shapes=[pltpu.VMEM((8, LANES), f32)],
        compiler_params=_params(("arbitrary",)),
        name="route",
    )(logits)


def _token_copy(src_ref, src_tok, dst_ref, dst_tok, sem, ns):
    src = src_ref.at[pl.ds(pl.multiple_of(src_tok * ns, ns), ns), :]
    dst = dst_ref.at[pl.ds(pl.multiple_of(dst_tok * ns, ns), ns), :]
    return pltpu.make_async_copy(src, dst, sem)


def _dispatch_kernel(pad_ref, pos_ref, hf_ref, xs_ref, zero_ref, sem, *, tm, pad_rows):
    i = pl.program_id(0)
    ns = hf_ref.shape[0] // tm

    @pl.when(i == 0)
    def _():
        zero_ref[...] = jnp.zeros_like(zero_ref)

        def zero_fill(e):
            dst = xs_ref.at[pl.ds(pl.multiple_of(pad_ref[e] * ns, ns), pad_rows * ns), :]
            return pltpu.make_async_copy(zero_ref, dst, sem)

        def zero_tail(t):
            rows = pad_rows * ns
            return pltpu.make_async_copy(zero_ref, xs_ref.at[pl.ds(xs_ref.shape[0] - (t + 1) * rows, rows), :], sem)

        for cp in [zero_tail(t) for t in range(N_EXPERTS + 1)] + [zero_fill(e) for e in range(N_EXPERTS)]:
            cp.start()
            cp.wait()

    def start(r, carry):
        _token_copy(hf_ref, r, xs_ref, pos_ref[0, 0, 2 * r], sem, ns).start(priority=0)
        _token_copy(hf_ref, r, xs_ref, pos_ref[0, 0, 2 * r + 1], sem, ns).start(priority=1)
        return carry

    lax.fori_loop(0, tm, start, 0, unroll=ROW_DMA_UNROLL)

    def wait(r, carry):
        _token_copy(hf_ref, r, xs_ref, pos_ref[0, 0, 2 * r], sem, ns).wait()
        _token_copy(hf_ref, r, xs_ref, pos_ref[0, 0, 2 * r + 1], sem, ns).wait()
        return carry

    lax.fori_loop(0, tm, wait, 0, unroll=ROW_DMA_UNROLL)


def _dispatch(pad_start, pos, hf, n_sorted_rows, ns):
    tm = TOKEN_TILE
    n_tiles = hf.shape[0] // (tm * ns)
    return pl.pallas_call(
        functools.partial(_dispatch_kernel, tm=tm, pad_rows=EXPERT_TILE),
        out_shape=jax.ShapeDtypeStruct((n_sorted_rows * ns, LANES), f32),
        grid_spec=pltpu.PrefetchScalarGridSpec(
            num_scalar_prefetch=1,
            grid=(n_tiles,),
            in_specs=[
                pl.BlockSpec((1, 1, 2 * tm), lambda i, pad: (i, 0, 0), memory_space=pltpu.SMEM),
                pl.BlockSpec((tm * ns, LANES), lambda i, pad: (i, 0)),
            ],
            out_specs=pl.BlockSpec(memory_space=pl.ANY),
            scratch_shapes=[pltpu.VMEM((EXPERT_TILE * ns, LANES), f32), pltpu.SemaphoreType.DMA(())],
        ),
        compiler_params=_params(("arbitrary",)),
        name="dispatch",
    )(pad_start, pos.reshape(n_tiles, 1, 2 * tm), hf)


def _combine_kernel(pos_ref, pos_next_ref, y_ref, rt_ref, x_ref, mod_ref, g_ref, o_ref, buf0, buf1, sems):
    i = pl.program_id(0)
    tm = x_ref.shape[0]
    ns = buf0.shape[1] // tm
    slot = i % 2

    def copies(p_ref, sl, r):
        return (_token_copy(y_ref, p_ref[0, 0, 2 * r], buf0.at[sl], r, sems.at[sl], ns),
                _token_copy(y_ref, p_ref[0, 0, 2 * r + 1], buf1.at[sl], r, sems.at[sl], ns))

    def issue(p_ref, sl):
        def start(r, carry):
            for priority, cp in enumerate(copies(p_ref, sl, r)):
                cp.start(priority=priority)
            return carry

        lax.fori_loop(0, tm, start, 0, unroll=ROW_DMA_UNROLL)

    @pl.when(i == 0)
    def _():
        issue(pos_ref, slot)

    @pl.when(i + 1 < pl.num_programs(0))
    def _():
        issue(pos_next_ref, 1 - slot)

    def wait(r, carry):
        for cp in copies(pos_ref, slot, r):
            cp.wait()
        return carry

    lax.fori_loop(0, tm, wait, 0, unroll=ROW_DMA_UNROLL)

    rt = rt_ref[...]
    y = (rt[:, 4:5] * _load_token_tiles(buf0.at[slot], tm)
         + rt[:, 5:6] * _load_token_tiles(buf1.at[slot], tm))
    m = mod_ref[0]
    o_ref[...] = x_ref[...] + m[5:6, :] * _rms(y, g_ref[0, 3:4, :])


def _combine(pos, y, routed, x, mod_l, g_l, *, tiles_per_batch, n_batch):
    n_rows, d = x.shape
    tm = TOKEN_TILE
    n_tiles = n_rows // tm
    ns = d // LANES
    pos3 = pos.reshape(n_tiles, 1, 2 * tm)
    return pl.pallas_call(
        _combine_kernel,
        out_shape=jax.ShapeDtypeStruct((n_rows, d), f32),
        grid=(n_tiles,),
        in_specs=[
            pl.BlockSpec((1, 1, 2 * tm), lambda i: (i, 0, 0), memory_space=pltpu.SMEM),
            pl.BlockSpec((1, 1, 2 * tm), lambda i: (jnp.minimum(i + 1, n_tiles - 1), 0, 0),
                         memory_space=pltpu.SMEM),
            pl.BlockSpec(memory_space=pl.ANY),
            pl.BlockSpec((tm, LANES), lambda i: (i, 0)),
            pl.BlockSpec((tm, d), lambda i: (i, 0)),
            pl.BlockSpec((1, N_MOD, d), lambda i: (jnp.minimum(i // tiles_per_batch, n_batch), 0, 0)),
            pl.BlockSpec((1, 4, d), lambda i: (0, 0, 0)),
        ],
        out_specs=pl.BlockSpec((tm, d), lambda i: (i, 0)),
        scratch_shapes=[pltpu.VMEM((2, tm * ns, LANES), f32), pltpu.VMEM((2, tm * ns, LANES), f32),
                        pltpu.SemaphoreType.DMA((2,))],
        compiler_params=_params(("arbitrary",)),
        name="combine",
    )(pos3, pos3, y, routed, x, mod_l, g_l)


def _rope_tables(seq):
    half = HEAD_DIM // 4
    freqs = ROPE_BASE ** (-jnp.arange(half, dtype=f32) / half)
    t = jnp.arange(seq)
    row = (t // GRID_W).astype(f32)[:, None] * freqs[None, :]
    col = (t % GRID_W).astype(f32)[:, None] * freqs[None, :]
    cos = jnp.concatenate([jnp.cos(row), jnp.cos(row), jnp.cos(col), jnp.cos(col)], axis=1)
    sin = jnp.concatenate([-jnp.sin(row), jnp.sin(row), -jnp.sin(col), jnp.sin(col)], axis=1)
    reps = LANES // HEAD_DIM
    return jnp.tile(cos, (1, reps)), jnp.tile(sin, (1, reps))


def _head_perm():
    cols = []
    for hq in HEAD_ORDER:
        cols.extend(range(hq * HEAD_DIM, (hq + 1) * HEAD_DIM))
    return jnp.array(cols, dtype=jnp.int32)


def _moe_plan(routed, counts, n_tiles_max):
    te_rows = EXPERT_TILE
    cnt = counts[0, :N_EXPERTS].astype(jnp.int32)
    padded = ((cnt + te_rows - 1) // te_rows) * te_rows
    ends = jnp.cumsum(padded)
    starts = ends - padded
    idx = routed[:, 0:2].astype(jnp.int32)
    rank = routed[:, 2:4].astype(jnp.int32)
    onehot = idx[:, :, None] == jnp.arange(N_EXPERTS, dtype=jnp.int32)[None, None, :]
    pos = rank + jnp.sum(jnp.where(onehot, starts[None, None, :], 0), axis=-1)
    tile_start = jnp.arange(n_tiles_max, dtype=jnp.int32) * te_rows
    tile_expert = jnp.minimum(
        jnp.sum((tile_start[:, None] >= ends[None, :]).astype(jnp.int32), axis=1), N_EXPERTS - 1)
    n_valid = (ends[-1] // te_rows).astype(jnp.int32).reshape(1)
    pad_start = (starts + cnt).astype(jnp.int32)
    return pos.reshape(-1), tile_expert.astype(jnp.int32), n_valid, pad_start


def kernel(x, c, ctx, c_ctx, w_mod, b_mod, norm_g, w_in, sinks, w_att_o, pool_w, pool_scale, w_pool_o,
           w_out, w_ffn_gate, w_ffn_up, w_ffn_down, w_router, b_router, w_exp_gate, w_exp_up,
           w_exp_down):
    n_batch, seq, d = x.shape
    n_ctx = ctx.shape[1]
    depth = w_mod.shape[0]
    tm = TOKEN_TILE
    n_lat = n_batch * seq
    t_all = n_lat + n_batch * n_ctx
    assert seq % tm == 0 and (n_batch * n_ctx) % tm == 0 and n_lat % n_ctx == 0
    assert (n_ctx % tm == 0) or (tm % n_ctx == 0)
    n_lat_tiles = n_lat // tm
    tiles_per_batch = seq // tm
    geom = dict(n_lat_tiles=n_lat_tiles, tiles_per_batch=tiles_per_batch, n_batch=n_batch)

    c8 = jnp.zeros((8, d), f32).at[:n_batch].set(c).at[n_batch].set(c_ctx)
    mod = _modulation(c8, w_mod, b_mod).reshape(depth, 8, N_MOD, d)
    cos, sin = _rope_tables(seq)
    perm = _head_perm()

    v0 = ATT_WIDTH + KV_WIDTH
    qkp_cols = jnp.concatenate([perm, jnp.arange(ATT_WIDTH, v0, dtype=jnp.int32),
                                jnp.arange(v0 + KV_WIDTH, QKVP_WIDTH, dtype=jnp.int32)])
    w_qkp = w_in[:, :, qkp_cols].astype(bf16)
    w_vt = jnp.swapaxes(w_in[:, :, v0:v0 + KV_WIDTH], 1, 2).astype(bf16)
    w_gate = (0.5 * w_in[:, :, QKVP_WIDTH:]).astype(bf16)
    w_att = w_att_o[:, perm, :].astype(bf16)
    w_pool = w_pool_o.astype(bf16)
    w_o = (0.5 * w_out).astype(bf16)
    poolw = pool_w.astype(bf16)
    sinks_p = sinks[:, jnp.array(HEAD_ORDER)]
    w_r = jnp.zeros((w_router.shape[0], d, LANES), bf16).at[:, :, :N_EXPERTS].set(w_router.astype(bf16))
    b_r = jnp.zeros((b_router.shape[0], 1, LANES), f32).at[:, 0, :N_EXPERTS].set(b_router)
    wd_g, wd_u, wd_d = (w.astype(bf16) for w in (w_ffn_gate, w_ffn_up, w_ffn_down))
    we_g, we_u, we_d = (w.astype(bf16) for w in (w_exp_gate, w_exp_up, w_exp_down))

    xa = jnp.concatenate([x.reshape(n_lat, d), ctx.reshape(n_batch * n_ctx, d)], axis=0)

    n_e = w_exp_gate.shape[1]
    we_g, we_u, we_d = (w.reshape((-1,) + w.shape[2:]) for w in (we_g, we_u, we_d))
    pscale = pool_scale[:, None, :]
    ns = d // LANES

    for l in range(depth):
        last = l == depth - 1
        moe = l % 2 == 1
        n_rows = n_lat if last else t_all
        mod_l = mod[l]
        g_l = norm_g[l:l + 1]

        q, k, vt, p = _inproj(xa, mod_l, g_l, cos, sin, l, w_qkp, w_vt, **geom)
        att = _attention(sinks_p[l], q, k, vt, n_batch=n_batch, seq=seq, n_ctx=n_ctx, with_ctx=not last)
        router = (w_r, b_r) if moe else None
        outs = _merge(xa, att, p, mod_l, g_l, l, w_gate, w_att, w_pool, w_o, poolw, pscale, router,
                      n_rows=n_rows, seq=seq, n_ctx=n_ctx, **geom)
        i = l // 2
        if not moe:
            x1, hf = outs
            n_tiles = n_rows // tm
            xa = _ffn(jnp.full((n_tiles,), i, jnp.int32), jnp.full((1,), n_tiles, jnp.int32), hf,
                      wd_g, wd_u, wd_d, (x1, mod_l, g_l, tiles_per_batch, n_batch),
                      tm=tm, n_chunks=1, tiled_rows=False)
        else:
            x1, hf, logits = outs
            routed, counts = _route(logits)
            n_tiles_max = (2 * n_rows) // EXPERT_TILE + N_EXPERTS
            pos, tile_expert, n_valid, pad_start = _moe_plan(routed, counts, n_tiles_max)
            xs = _dispatch(pad_start, pos, hf, (n_tiles_max + 1) * EXPERT_TILE, ns)
            y = _ffn(tile_expert + i * n_e, n_valid, xs, we_g, we_u, we_d, None,
                     tm=EXPERT_TILE, n_chunks=2, tiled_rows=True)
            xa = _combine(pos, y, routed, x1, mod_l, g_l, tiles_per_batch=tiles_per_batch,
                          n_batch=n_batch)
    return xa[:n_lat].reshape(n_batch, seq, d)
```

```python
import functools

import jax
import jax.numpy as jnp
from jax import lax
from jax.experimental import pallas as pl
from jax.experimental.pallas import tpu as pltpu

GRID_W = 64
HEAD_DIM = 64
N_Q_HEADS = 8
N_KV_HEADS = 2
Q_PER_KV = N_Q_HEADS // N_KV_HEADS
ATT_WIDTH = N_Q_HEADS * HEAD_DIM
KV_WIDTH = N_KV_HEADS * HEAD_DIM
WINDOW = 128
BLOCK = 128
POOL_WINDOWS = (2, 4, 8, 16)
N_POOL_GROUPS = 4
POOL_GROUP = 128
POOL_WIDTH = N_POOL_GROUPS * POOL_GROUP
POOL_HALO = 8
N_MOD = 6
N_EXPERTS = 8
ROPE_BASE = 10000.0
EPS = 1e-6
NEG_INF = -1e30
LOG2_E = 1.4426950408889634
LANES = 128
QKVP_WIDTH = ATT_WIDTH + 2 * KV_WIDTH + POOL_WIDTH

TOKEN_TILE = 512
EXPERT_TILE = 1024
FF_SUB = 256
Q_GROUP = 4
ROW_DMA_UNROLL = 8
VMEM_LIMIT = 56 * 1024 * 1024

HEAD_ORDER = (0, 4, 1, 5, 2, 6, 3, 7)

f32 = jnp.float32
bf16 = jnp.bfloat16


def _params(sem, vmem=VMEM_LIMIT):
    return pltpu.CompilerParams(dimension_semantics=sem, vmem_limit_bytes=vmem)


def _load_token_tiles(ref, tm):
    ns = ref.shape[0] // tm
    return jnp.concatenate([ref[pl.ds(s, tm, stride=ns), :] for s in range(ns)], axis=1)


def _store_token_tiles(ref, val):
    tm = val.shape[0]
    ns = ref.shape[0] // tm
    for s in range(ns):
        ref[pl.ds(s, tm, stride=ns), :] = val[:, s * LANES:(s + 1) * LANES]


def _rms(x, g):
    ms = jnp.mean(x * x, axis=-1, keepdims=True)
    return x * lax.rsqrt(ms + EPS) * g


def _mod_kernel(c_ref, w_ref, b_ref, o_ref):
    c = c_ref[...]
    sc = (c * jax.nn.sigmoid(c)).astype(bf16)
    o_ref[0] = jnp.dot(sc, w_ref[0].astype(bf16), preferred_element_type=f32) + b_ref[0]


def _modulation(c8, w_mod, b_mod):
    depth, d, n = w_mod.shape
    tn = n // 4
    return pl.pallas_call(
        _mod_kernel,
        out_shape=jax.ShapeDtypeStruct((depth, 8, n), f32),
        grid=(depth, n // tn),
        in_specs=[
            pl.BlockSpec((8, d), lambda l, j: (0, 0)),
            pl.BlockSpec((1, d, tn), lambda l, j: (l, 0, j)),
            pl.BlockSpec((1, 1, tn), lambda l, j: (l, 0, j)),
        ],
        out_specs=pl.BlockSpec((1, 8, tn), lambda l, j: (l, 0, j)),
        compiler_params=_params(("arbitrary", "arbitrary")),
        name="modulation",
    )(c8, w_mod, b_mod.reshape(depth, 1, n))


def _inproj_kernel(x_ref, mod_ref, g_ref, cos_ref, sin_ref, w_ref, wvt_ref, q_ref, k_ref, vt_ref, p_ref,
                   *, n_lat_tiles):
    i = pl.program_id(0)
    m = mod_ref[0]
    h = _rms(x_ref[...], g_ref[0, 0:1, :])
    h = (h * (1.0 + m[1:2, :]) + m[0:1, :]).astype(bf16)
    u = jnp.dot(h, w_ref[...], preferred_element_type=f32)
    vt = lax.dot_general(wvt_ref[...], h, (((1,), (1,)), ((), ())), preferred_element_type=f32)
    vt_ref[...] = vt.astype(bf16)

    is_lat = i < n_lat_tiles
    cos = jnp.where(is_lat, cos_ref[...], 1.0)
    sin = jnp.where(is_lat, sin_ref[...], 0.0)
    lane = lax.broadcasted_iota(jnp.int32, (1, LANES), 1)
    first_half = (lane % 32) < 16

    def rope(t):
        partner = jnp.where(first_half, pltpu.roll(t, LANES - 16, axis=1), pltpu.roll(t, 16, axis=1))
        return t * cos + partner * sin

    for jb in range(ATT_WIDTH // LANES):
        blk = rope(u[:, jb * LANES:(jb + 1) * LANES])
        q_ref[:, jb * LANES:(jb + 1) * LANES] = (blk * (LOG2_E * HEAD_DIM ** -0.5)).astype(bf16)
    o = ATT_WIDTH
    k_ref[...] = rope(u[:, o:o + KV_WIDTH]).astype(bf16)
    o += KV_WIDTH
    p_ref[...] = u[:, o:o + POOL_WIDTH]


def _inproj(x, mod_l, g_l, cos, sin, layer, w, wvt, *, n_lat_tiles, tiles_per_batch, n_batch):
    t_all, d = x.shape
    tm = TOKEN_TILE
    n_tiles = t_all // tm

    def mod_idx(i):
        return (jnp.minimum(i // tiles_per_batch, n_batch), 0, 0)

    def tab_idx(i):
        return (jnp.where(i < n_lat_tiles, i % tiles_per_batch, 0), 0)

    return pl.pallas_call(
        functools.partial(_inproj_kernel, n_lat_tiles=n_lat_tiles),
        out_shape=(
            jax.ShapeDtypeStruct((t_all, ATT_WIDTH), bf16),
            jax.ShapeDtypeStruct((t_all, KV_WIDTH), bf16),
            jax.ShapeDtypeStruct((KV_WIDTH, t_all), bf16),
            jax.ShapeDtypeStruct((t_all, POOL_WIDTH), f32),
        ),
        grid=(n_tiles,),
        in_specs=[
            pl.BlockSpec((tm, d), lambda i: (i, 0)),
            pl.BlockSpec((1, N_MOD, d), mod_idx),
            pl.BlockSpec((1, 4, d), lambda i: (0, 0, 0)),
            pl.BlockSpec((tm, LANES), tab_idx),
            pl.BlockSpec((tm, LANES), tab_idx),
            pl.BlockSpec((None,) + w.shape[1:], lambda i: (layer, 0, 0)),
            pl.BlockSpec((None,) + wvt.shape[1:], lambda i: (layer, 0, 0)),
        ],
        out_specs=(
            pl.BlockSpec((tm, ATT_WIDTH), lambda i: (i, 0)),
            pl.BlockSpec((tm, KV_WIDTH), lambda i: (i, 0)),
            pl.BlockSpec((KV_WIDTH, tm), lambda i: (0, i)),
            pl.BlockSpec((tm, POOL_WIDTH), lambda i: (i, 0)),
        ),
        compiler_params=_params(("parallel",)),
        name="inproj",
    )(x, mod_l, g_l, cos, sin, w, wvt)


def _attn_kernel(sink_ref, q_ref, kp_ref, kc_ref, kn_ref, kx_ref, vp_ref, vc_ref, vn_ref, vx_ref,
                 o_ref, *, nb):
    i = pl.program_id(1)
    nq = ATT_WIDTH // LANES

    lo_lane = lax.broadcasted_iota(jnp.int32, (1, LANES), 1) < HEAD_DIM
    lo_row = lax.broadcasted_iota(jnp.int32, (KV_WIDTH, 1), 0) < HEAD_DIM

    def attend(k_refs, v_refs, biases):
        kcat = jnp.concatenate([r[...] for r in k_refs], axis=0)
        vcat = jnp.concatenate([r[...] for r in v_refs], axis=1)
        nk = kcat.shape[0]
        zk, zv = jnp.zeros_like(kcat), jnp.zeros_like(vcat)
        kk = jnp.concatenate([jnp.where(lo_lane, kcat, zk), jnp.where(lo_lane, zk, kcat)], axis=0)
        vvt = jnp.concatenate([jnp.where(lo_row, vcat, zv), jnp.where(lo_row, zv, vcat)], axis=1)
        orow = lax.broadcasted_iota(jnp.int32, (16, 2 * nk), 0) < 8
        ocol = lax.broadcasted_iota(jnp.int32, (16, 2 * nk), 1) < nk
        vvt = jnp.concatenate([vvt, jnp.where(orow == ocol, 1.0, 0.0).astype(bf16)], axis=0)

        for j0 in range(0, nq, Q_GROUP):
            js = range(j0, j0 + Q_GROUP)
            qg = jnp.concatenate([q_ref[:, j * LANES:(j + 1) * LANES] for j in js], axis=0)
            st = lax.dot_general(kk, qg, (((1,), (1,)), ((), ())), preferred_element_type=f32)
            ps, sink_e = [], []
            for half in range(2):
                pieces, off = [], half * nk
                for r, bias in zip(k_refs, biases):
                    n = r.shape[0]
                    blk = st[off:off + n, :]
                    pieces.append(blk if bias is None else blk + bias)
                    off += n
                sink = jnp.concatenate(
                    [jnp.full((1, BLOCK), sink_ref[2 * j + half] * LOG2_E, f32) for j in js], axis=1)
                mx = sink
                for blk in pieces:
                    mx = jnp.maximum(mx, jnp.max(blk, axis=0, keepdims=True))
                ps.extend(jnp.exp2(blk - mx).astype(bf16) for blk in pieces)
                sink_e.append(jnp.exp2(sink - mx))
            ot = jnp.dot(vvt, jnp.concatenate(ps, axis=0), preferred_element_type=f32)
            inv0 = 1.0 / (ot[KV_WIDTH:KV_WIDTH + 1, :] + sink_e[0])
            inv1 = 1.0 / (ot[KV_WIDTH + 8:KV_WIDTH + 9, :] + sink_e[1])
            o = ot[:KV_WIDTH, :] * jnp.where(lo_row, inv0, inv1)
            for n, j in enumerate(js):
                o_ref[:, j * LANES:(j + 1) * LANES] = o[:, n * BLOCK:(n + 1) * BLOCK].T.astype(bf16)

    @pl.when(i < nb)
    def _():
        c = lax.broadcasted_iota(jnp.int32, (BLOCK, BLOCK), 0)
        r = lax.broadcasted_iota(jnp.int32, (BLOCK, BLOCK), 1)
        b_prev = jnp.where(jnp.logical_and(c >= r, i >= 1), 0.0, NEG_INF)
        b_next = jnp.where(jnp.logical_and(c <= r, i <= nb - 2), 0.0, NEG_INF)
        b_prev = jnp.concatenate([b_prev] * Q_GROUP, axis=1)
        b_next = jnp.concatenate([b_next] * Q_GROUP, axis=1)
        attend((kp_ref, kc_ref, kn_ref, kx_ref), (vp_ref, vc_ref, vn_ref, vx_ref),
               (b_prev, None, b_next, None))

    @pl.when(i >= nb)
    def _():
        attend((kx_ref,), (vx_ref,), (None,))


def _attention(sinks, q, k, vt, *, n_batch, seq, n_ctx, with_ctx):
    nb = seq // BLOCK
    cb = n_ctx // BLOCK
    n_steps = nb + (cb if with_ctx else 0)
    t_out = n_batch * (seq + (n_ctx if with_ctx else 0))
    ctx_blk0 = (n_batch * seq) // n_ctx

    def q_idx(b, i):
        return (jnp.where(i < nb, b * nb + i, n_batch * nb + b * cb + (i - nb)), 0)

    def prev_blk(b, i):
        return b * nb + jnp.clip(i - 1, 0, nb - 1)

    def cur_blk(b, i):
        return b * nb + jnp.minimum(i, nb - 1)

    def next_blk(b, i):
        return b * nb + jnp.minimum(i + 1, nb - 1)

    k_specs = [
        pl.BlockSpec((BLOCK, KV_WIDTH), lambda b, i: (prev_blk(b, i), 0)),
        pl.BlockSpec((BLOCK, KV_WIDTH), lambda b, i: (cur_blk(b, i), 0)),
        pl.BlockSpec((BLOCK, KV_WIDTH), lambda b, i: (next_blk(b, i), 0)),
        pl.BlockSpec((n_ctx, KV_WIDTH), lambda b, i: (ctx_blk0 + b, 0)),
    ]
    vt_specs = [
        pl.BlockSpec((KV_WIDTH, BLOCK), lambda b, i: (0, prev_blk(b, i))),
        pl.BlockSpec((KV_WIDTH, BLOCK), lambda b, i: (0, cur_blk(b, i))),
        pl.BlockSpec((KV_WIDTH, BLOCK), lambda b, i: (0, next_blk(b, i))),
        pl.BlockSpec((KV_WIDTH, n_ctx), lambda b, i: (0, ctx_blk0 + b)),
    ]
    return pl.pallas_call(
        functools.partial(_attn_kernel, nb=nb),
        out_shape=jax.ShapeDtypeStruct((t_out, ATT_WIDTH), bf16),
        grid=(n_batch, n_steps),
        in_specs=[pl.BlockSpec(memory_space=pltpu.SMEM),
                  pl.BlockSpec((BLOCK, ATT_WIDTH), q_idx)] + k_specs + vt_specs,
        out_specs=pl.BlockSpec((BLOCK, ATT_WIDTH), q_idx),
        compiler_params=_params(("parallel", "parallel")),
        name="attention",
    )(sinks, q, k, k, k, k, vt, vt, vt, vt)


def _pool(p_scr, pc_ref, pooled_scr, row_pos, seq_len, edge_rows, tm):
    h = POOL_HALO

    def ext(lo, n, lanes):
        return p_scr[lo:lo + n, lanes]

    for gi, w in enumerate(POOL_WINDOWS):
        lanes = slice(gi * POOL_GROUP, (gi + 1) * POOL_GROUP)
        if w == 2:
            sums = ext(7, tm, lanes) + ext(8, tm, lanes)
        elif w == 4:
            sums = (ext(6, tm, lanes) + ext(7, tm, lanes)) + (ext(8, tm, lanes) + ext(9, tm, lanes))
        elif w == 8:
            a2 = ext(4, tm + 16, lanes) + ext(5, tm + 16, lanes)
            b2 = a2[0:tm + 8] + a2[2:tm + 10]
            sums = b2[0:tm] + b2[4:tm + 4]
        else:
            a3 = ext(0, tm + 24, lanes) + ext(1, tm + 24, lanes)
            b3 = a3[0:tm + 16] + a3[2:tm + 18]
            c3 = b3[0:tm + 8] + b3[4:tm + 12]
            sums = c3[0:tm] + c3[8:tm + 8]
        pooled_scr[:, lanes] = (sums * (1.0 / w) - pc_ref[:, lanes]).astype(pooled_scr.dtype)

        for g0 in edge_rows:
            pos = row_pos(g0)
            acc = jnp.zeros((h, POOL_GROUP), f32)
            for s in range(-(w // 2), w // 2):
                blk = ext(h + g0 + s, h, lanes)
                if s < 0:
                    blk = jnp.where(pos >= -s, blk, 0.0)
                elif s > 0:
                    blk = jnp.where(pos < seq_len - s, blk, 0.0)
                acc = acc + blk
            cnt = (jnp.minimum(pos + w // 2, seq_len) - jnp.maximum(pos - w // 2, 0)).astype(f32)
            pooled_scr[g0:g0 + h, lanes] = (acc / cnt - pc_ref[g0:g0 + h, lanes]).astype(pooled_scr.dtype)


def _merge_kernel(x_ref, att_ref, pp_ref, pc_ref, pn_ref, mod_ref, g_ref, wgate_ref, watt_ref,
                  wpool_ref, wout_ref, poolw_ref, pscale_ref, *rest,
                  n_lat_tiles, tiles_per_batch, seq, n_ctx, with_router):
    if with_router:
        wr_ref, br_ref, x1_ref, hf_ref, lg_ref, p_scr, pooled_scr = rest
    else:
        x1_ref, hf_ref, p_scr, pooled_scr = rest
    i = pl.program_id(0)
    tm = x_ref.shape[0]
    d = x_ref.shape[1]
    m = mod_ref[0]
    g = g_ref[0]
    x = x_ref[...]

    h = _rms(x, g[0:1, :])
    h = (h * (1.0 + m[1:2, :]) + m[0:1, :]).astype(bf16)
    gate = 1.0 + jnp.tanh(jnp.dot(h, wgate_ref[...], preferred_element_type=f32))

    p_scr[0:POOL_HALO, :] = pp_ref[...]
    p_scr[POOL_HALO:POOL_HALO + tm, :] = pc_ref[...]
    p_scr[POOL_HALO + tm:2 * POOL_HALO + tm, :] = pn_ref[...]
    p_scr[2 * POOL_HALO + tm:, :] = jnp.zeros((2 * POOL_HALO, POOL_WIDTH), f32)
    is_lat = i < n_lat_tiles
    tile_pos = jnp.where(is_lat, (i % tiles_per_batch) * tm, (i - n_lat_tiles) * tm)
    seq_len = jnp.where(is_lat, seq, n_ctx)
    r8 = lax.broadcasted_iota(jnp.int32, (POOL_HALO, POOL_GROUP), 0)

    def row_pos(g0):
        return jnp.where(is_lat, tile_pos + g0 + r8, (tile_pos + g0 + r8) % n_ctx)

    edge_rows = {0, tm - POOL_HALO}
    for bnd in range(n_ctx, tm, n_ctx):
        edge_rows |= {bnd - POOL_HALO, bnd}
    _pool(p_scr, pc_ref, pooled_scr, row_pos, seq_len, sorted(edge_rows), tm)

    mixed = []
    for gi in range(N_POOL_GROUPS):
        lanes = slice(gi * POOL_GROUP, (gi + 1) * POOL_GROUP)
        mixed.append((jnp.dot(pooled_scr[:, lanes], poolw_ref[gi], preferred_element_type=f32)
                      * pscale_ref[:, lanes]).astype(bf16))
    pool_acc = jnp.dot(jnp.concatenate(mixed, axis=1), wpool_ref[...], preferred_element_type=f32)

    a = jnp.dot(att_ref[...], watt_ref[...], preferred_element_type=f32)
    merged = (gate[:, :d] * a + gate[:, d:] * pool_acc).astype(bf16)
    y = jnp.dot(merged, wout_ref[...], preferred_element_type=f32)
    x1 = x + m[2:3, :] * _rms(y, g[1:2, :])
    x1_ref[...] = x1
    hf = _rms(x1, g[2:3, :]) * (1.0 + m[4:5, :]) + m[3:4, :]
    if with_router:
        _store_token_tiles(hf_ref, hf)
        lg_ref[...] = jnp.dot(hf.astype(bf16), wr_ref[...], preferred_element_type=f32) + br_ref[...]
    else:
        hf_ref[...] = hf.astype(bf16)


def _merge(x, att, p, mod_l, g_l, layer, wgate, watt, wpool, wout, poolw, pscale, router,
           *, n_rows, n_lat_tiles, tiles_per_batch, n_batch, seq, n_ctx):
    d = x.shape[1]
    tm = TOKEN_TILE
    n_tiles = n_rows // tm
    hb = tm // POOL_HALO
    last_halo = p.shape[0] // POOL_HALO - 1
    with_router = router is not None

    def mod_idx(i):
        return (jnp.minimum(i // tiles_per_batch, n_batch), 0, 0)

    def layer_spec(w, idx=None):
        idx = layer if idx is None else idx
        zeros = (0,) * (w.ndim - 1)
        return pl.BlockSpec((None,) + w.shape[1:], lambda i: (idx,) + zeros)

    in_specs = [
        pl.BlockSpec((tm, d), lambda i: (i, 0)),
        pl.BlockSpec((tm, ATT_WIDTH), lambda i: (i, 0)),
        pl.BlockSpec((POOL_HALO, POOL_WIDTH), lambda i: (jnp.maximum(i * hb - 1, 0), 0)),
        pl.BlockSpec((tm, POOL_WIDTH), lambda i: (i, 0)),
        pl.BlockSpec((POOL_HALO, POOL_WIDTH), lambda i: (jnp.minimum((i + 1) * hb, last_halo), 0)),
        pl.BlockSpec((1, N_MOD, d), mod_idx),
        pl.BlockSpec((1, 4, d), lambda i: (0, 0, 0)),
        layer_spec(wgate), layer_spec(watt), layer_spec(wpool), layer_spec(wout), layer_spec(poolw),
        layer_spec(pscale),
    ]
    args = [x, att, p, p, p, mod_l, g_l, wgate, watt, wpool, wout, poolw, pscale]
    out_shape = [jax.ShapeDtypeStruct((n_rows, d), f32)]
    out_specs = [pl.BlockSpec((tm, d), lambda i: (i, 0))]
    if with_router:
        wr, br = router
        in_specs += [layer_spec(wr, layer // 2), layer_spec(br, layer // 2)]
        args += [wr, br]
        ns = d // LANES
        out_shape += [jax.ShapeDtypeStruct((n_rows * ns, LANES), f32),
                      jax.ShapeDtypeStruct((n_rows, LANES), f32)]
        out_specs += [pl.BlockSpec((tm * ns, LANES), lambda i: (i, 0)),
                      pl.BlockSpec((tm, LANES), lambda i: (i, 0))]
    else:
        out_shape.append(jax.ShapeDtypeStruct((n_rows, d), bf16))
        out_specs.append(pl.BlockSpec((tm, d), lambda i: (i, 0)))
    return pl.pallas_call(
        functools.partial(_merge_kernel, n_lat_tiles=n_lat_tiles, tiles_per_batch=tiles_per_batch,
                          seq=seq, n_ctx=n_ctx, with_router=with_router),
        out_shape=tuple(out_shape),
        grid=(n_tiles,),
        in_specs=in_specs,
        out_specs=tuple(out_specs),
        scratch_shapes=[pltpu.VMEM((tm + 4 * POOL_HALO, POOL_WIDTH), f32),
                        pltpu.VMEM((tm, POOL_WIDTH), bf16)],
        compiler_params=_params(("parallel",)),
        name="merge",
    )(*args)


def _ff_splits(fc):
    edges = list(range(0, fc, FF_SUB)) + [fc]
    return tuple(zip(edges[:-1], edges[1:]))


def _ffn_kernel(te_ref, nv_ref, x_ref, wg_ref, wu_ref, wd_ref, *rest, n_chunks, fuse_residual, tiled_rows):
    if fuse_residual:
        xres_ref, mod_ref, g_ref, o_ref, acc_ref = rest
    else:
        o_ref, acc_ref = rest
    i = pl.program_id(0)
    j = pl.program_id(1)

    @pl.when(i < nv_ref[0])
    def _():
        @pl.when(j == 0)
        def _():
            acc_ref[...] = jnp.zeros_like(acc_ref)

        tm = acc_ref.shape[0]
        xb = (_load_token_tiles(x_ref, tm) if tiled_rows else x_ref[...]).astype(bf16)
        for c0, c1 in _ff_splits(wg_ref.shape[1]):
            gt = jnp.dot(xb, wg_ref[:, c0:c1], preferred_element_type=f32)
            up = jnp.dot(xb, wu_ref[:, c0:c1], preferred_element_type=f32)
            act = (gt * jax.nn.sigmoid(gt) * up).astype(bf16)
            acc_ref[...] += jnp.dot(act, wd_ref[c0:c1, :], preferred_element_type=f32)

        @pl.when(j == n_chunks - 1)
        def _():
            y = acc_ref[...]
            if fuse_residual:
                m = mod_ref[0]
                y = xres_ref[...] + m[5:6, :] * _rms(y, g_ref[0, 3:4, :])
            if tiled_rows:
                _store_token_tiles(o_ref, y)
            else:
                o_ref[...] = y

    @pl.when(jnp.logical_and(i >= nv_ref[0], j == n_chunks - 1))
    def _():
        o_ref[...] = jnp.zeros_like(o_ref)


def _ffn(tile_expert, n_valid, x, wg, wu, wd, residual, *, tm, n_chunks, tiled_rows):
    d = wg.shape[1]
    ns = d // LANES if tiled_rows else 1
    n_rows = x.shape[0] // ns
    n_tiles = n_rows // tm
    ff = wg.shape[2]
    fc = ff // n_chunks
    fuse = residual is not None
    row_block = (tm * ns, LANES) if tiled_rows else (tm, d)

    def row_idx(i, j, te, nv):
        return (jnp.minimum(i, nv[0] - 1), 0)

    def w_in_idx(i, j, te, nv):
        return (te[jnp.minimum(i, nv[0] - 1)], 0, j)

    def w_out_idx(i, j, te, nv):
        return (te[jnp.minimum(i, nv[0] - 1)], j, 0)

    in_specs = [
        pl.BlockSpec(row_block, row_idx),
        pl.BlockSpec((None, d, fc), w_in_idx),
        pl.BlockSpec((None, d, fc), w_in_idx),
        pl.BlockSpec((None, fc, d), w_out_idx),
    ]
    args = [x, wg, wu, wd]
    if fuse:
        xres, mod_l, g_l, tiles_per_batch, n_batch = residual
        in_specs += [
            pl.BlockSpec((tm, d), row_idx),
            pl.BlockSpec((1, N_MOD, d),
                         lambda i, j, te, nv: (jnp.minimum(i // tiles_per_batch, n_batch), 0, 0)),
            pl.BlockSpec((1, 4, d), lambda i, j, te, nv: (0, 0, 0)),
        ]
        args += [xres, mod_l, g_l]
    return pl.pallas_call(
        functools.partial(_ffn_kernel, n_chunks=n_chunks, fuse_residual=fuse, tiled_rows=tiled_rows),
        out_shape=jax.ShapeDtypeStruct((n_rows * ns, row_block[1]), f32),
        grid_spec=pltpu.PrefetchScalarGridSpec(
            num_scalar_prefetch=2,
            grid=(n_tiles, n_chunks),
            in_specs=in_specs,
            out_specs=pl.BlockSpec(row_block, lambda i, j, te, nv: (i, 0)),
            scratch_shapes=[pltpu.VMEM((tm, d), f32)],
        ),
        compiler_params=_params(("arbitrary", "arbitrary")),
        name="ffn",
    )(tile_expert, n_valid, *args)


def _route_kernel(lg_ref, out_ref, cnt_ref, carry_ref):
    i = pl.program_id(0)
    tm = lg_ref.shape[0]

    @pl.when(i == 0)
    def _():
        carry_ref[...] = jnp.zeros_like(carry_ref)

    lane = lax.broadcasted_iota(jnp.int32, (tm, LANES), 1).astype(f32)
    lg = jnp.where(lane < N_EXPERTS, lg_ref[...], -jnp.inf)
    m1 = jnp.max(lg, axis=1, keepdims=True)
    i1 = jnp.min(jnp.where(lg == m1, lane, float(LANES)), axis=1, keepdims=True)
    lg2 = jnp.where(lane == i1, -jnp.inf, lg)
    m2 = jnp.max(lg2, axis=1, keepdims=True)
    i2 = jnp.min(jnp.where(lg2 == m2, lane, float(LANES)), axis=1, keepdims=True)
    e2 = jnp.exp(m2 - m1)
    w1 = 1.0 / (1.0 + e2)
    w2 = e2 / (1.0 + e2)

    ind = jnp.where(lane == i1, 1.0, jnp.where(lane == i2, 1.0, 0.0))
    rr = lax.broadcasted_iota(jnp.int32, (tm, tm), 0)
    cc = lax.broadcasted_iota(jnp.int32, (tm, tm), 1)
    tri = jnp.where(rr > cc, 1.0, 0.0).astype(bf16)
    rank = jnp.dot(tri, ind.astype(bf16), preferred_element_type=f32) + carry_ref[0:1, :]
    r1 = jnp.sum(jnp.where(lane == i1, rank, 0.0), axis=1, keepdims=True)
    r2 = jnp.sum(jnp.where(lane == i2, rank, 0.0), axis=1, keepdims=True)

    fields = (i1, i2, r1, r2, w1, w2)
    packed = jnp.zeros((tm, LANES), f32)
    for n, fld in enumerate(fields):
        packed = jnp.where(lane == float(n), fld, packed)
    out_ref[...] = packed

    carry_ref[...] = carry_ref[...] + jnp.sum(ind, axis=0, keepdims=True)
    cnt_ref[...] = carry_ref[...]


def _route(logits):
    n_rows = logits.shape[0]
    tm = TOKEN_TILE
    return pl.pallas_call(
        _route_kernel,
        out_shape=(jax.ShapeDtypeStruct((n_rows, LANES), f32), jax.ShapeDtypeStruct((8, LANES), f32)),
        grid=(n_rows // tm,),
        in_specs=[pl.BlockSpec((tm, LANES), lambda i: (i, 0))],
        out_specs=(pl.BlockSpec((tm, LANES), lambda i: (i, 0)), pl.BlockSpec((8, LANES), lambda i: (0, 0))),
        scratch_shapes=[pltpu.VMEM((8, LANES), f32)],
        compiler_params=_params(("arbitrary",)),
        name="route",
    )(logits)


def _token_copy(src_ref, src_tok, dst_ref, dst_tok, sem, ns):
    src = src_ref.at[pl.ds(pl.multiple_of(src_tok * ns, ns), ns), :]
    dst = dst_ref.at[pl.ds(pl.multiple_of(dst_tok * ns, ns), ns), :]
    return pltpu.make_async_copy(src, dst, sem)


def _dispatch_kernel(pad_ref, pos_ref, hf_ref, xs_ref, zero_ref, sem, *, tm, pad_rows):
    i = pl.program_id(0)
    ns = hf_ref.shape[0] // tm

    @pl.when(i == 0)
    def _():
        zero_ref[...] = jnp.zeros_like(zero_ref)

        def zero_fill(e):
            dst = xs_ref.at[pl.ds(pl.multiple_of(pad_ref[e] * ns, ns), pad_rows * ns), :]
            return pltpu.make_async_copy(zero_ref, dst, sem)

        def zero_tail(t):
            rows = pad_rows * ns
            return pltpu.make_async_copy(zero_ref, xs_ref.at[pl.ds(xs_ref.shape[0] - (t + 1) * rows, rows), :], sem)

        for cp in [zero_tail(t) for t in range(N_EXPERTS + 1)] + [zero_fill(e) for e in range(N_EXPERTS)]:
            cp.start()
            cp.wait()

    def start(r, carry):
        _token_copy(hf_ref, r, xs_ref, pos_ref[0, 0, 2 * r], sem, ns).start(priority=0)
        _token_copy(hf_ref, r, xs_ref, pos_ref[0, 0, 2 * r + 1], sem, ns).start(priority=1)
        return carry

    lax.fori_loop(0, tm, start, 0, unroll=ROW_DMA_UNROLL)

    def wait(r, carry):
        _token_copy(hf_ref, r, xs_ref, pos_ref[0, 0, 2 * r], sem, ns).wait()
        _token_copy(hf_ref, r, xs_ref, pos_ref[0, 0, 2 * r + 1], sem, ns).wait()
        return carry

    lax.fori_loop(0, tm, wait, 0, unroll=ROW_DMA_UNROLL)


def _dispatch(pad_start, pos, hf, n_sorted_rows, ns):
    tm = TOKEN_TILE
    n_tiles = hf.shape[0] // (tm * ns)
    return pl.pallas_call(
        functools.partial(_dispatch_kernel, tm=tm, pad_rows=EXPERT_TILE),
        out_shape=jax.ShapeDtypeStruct((n_sorted_rows * ns, LANES), f32),
        grid_spec=pltpu.PrefetchScalarGridSpec(
            num_scalar_prefetch=1,
            grid=(n_tiles,),
            in_specs=[
                pl.BlockSpec((1, 1, 2 * tm), lambda i, pad: (i, 0, 0), memory_space=pltpu.SMEM),
                pl.BlockSpec((tm * ns, LANES), lambda i, pad: (i, 0)),
            ],
            out_specs=pl.BlockSpec(memory_space=pl.ANY),
            scratch_shapes=[pltpu.VMEM((EXPERT_TILE * ns, LANES), f32), pltpu.SemaphoreType.DMA(())],
        ),
        compiler_params=_params(("arbitrary",)),
        name="dispatch",
    )(pad_start, pos.reshape(n_tiles, 1, 2 * tm), hf)


def _combine_kernel(pos_ref, pos_next_ref, y_ref, rt_ref, x_ref, mod_ref, g_ref, o_ref, buf0, buf1, sems):
    i = pl.program_id(0)
    tm = x_ref.shape[0]
    ns = buf0.shape[1] // tm
    slot = i % 2

    def copies(p_ref, sl, r):
        return (_token_copy(y_ref, p_ref[0, 0, 2 * r], buf0.at[sl], r, sems.at[sl], ns),
                _token_copy(y_ref, p_ref[0, 0, 2 * r + 1], buf1.at[sl], r, sems.at[sl], ns))

    def issue(p_ref, sl):
        def start(r, carry):
            for priority, cp in enumerate(copies(p_ref, sl, r)):
                cp.start(priority=priority)
            return carry

        lax.fori_loop(0, tm, start, 0, unroll=ROW_DMA_UNROLL)

    @pl.when(i == 0)
    def _():
        issue(pos_ref, slot)

    @pl.when(i + 1 < pl.num_programs(0))
    def _():
        issue(pos_next_ref, 1 - slot)

    def wait(r, carry):
        for cp in copies(pos_ref, slot, r):
            cp.wait()
        return carry

    lax.fori_loop(0, tm, wait, 0, unroll=ROW_DMA_UNROLL)

    rt = rt_ref[...]
    y = (rt[:, 4:5] * _load_token_tiles(buf0.at[slot], tm)
         + rt[:, 5:6] * _load_token_tiles(buf1.at[slot], tm))
    m = mod_ref[0]
    o_ref[...] = x_ref[...] + m[5:6, :] * _rms(y, g_ref[0, 3:4, :])


def _combine(pos, y, routed, x, mod_l, g_l, *, tiles_per_batch, n_batch):
    n_rows, d = x.shape
    tm = TOKEN_TILE
    n_tiles = n_rows // tm
    ns = d // LANES
    pos3 = pos.reshape(n_tiles, 1, 2 * tm)
    return pl.pallas_call(
        _combine_kernel,
        out_shape=jax.ShapeDtypeStruct((n_rows, d), f32),
        grid=(n_tiles,),
        in_specs=[
            pl.BlockSpec((1, 1, 2 * tm), lambda i: (i, 0, 0), memory_space=pltpu.SMEM),
            pl.BlockSpec((1, 1, 2 * tm), lambda i: (jnp.minimum(i + 1, n_tiles - 1), 0, 0),
                         memory_space=pltpu.SMEM),
            pl.BlockSpec(memory_space=pl.ANY),
            pl.BlockSpec((tm, LANES), lambda i: (i, 0)),
            pl.BlockSpec((tm, d), lambda i: (i, 0)),
            pl.BlockSpec((1, N_MOD, d), lambda i: (jnp.minimum(i // tiles_per_batch, n_batch), 0, 0)),
            pl.BlockSpec((1, 4, d), lambda i: (0, 0, 0)),
        ],
        out_specs=pl.BlockSpec((tm, d), lambda i: (i, 0)),
        scratch_shapes=[pltpu.VMEM((2, tm * ns, LANES), f32), pltpu.VMEM((2, tm * ns, LANES), f32),
                        pltpu.SemaphoreType.DMA((2,))],
        compiler_params=_params(("arbitrary",)),
        name="combine",
    )(pos3, pos3, y, routed, x, mod_l, g_l)


def _rope_tables(seq):
    half = HEAD_DIM // 4
    freqs = ROPE_BASE ** (-jnp.arange(half, dtype=f32) / half)
    t = jnp.arange(seq)
    row = (t // GRID_W).astype(f32)[:, None] * freqs[None, :]
    col = (t % GRID_W).astype(f32)[:, None] * freqs[None, :]
    cos = jnp.concatenate([jnp.cos(row), jnp.cos(row), jnp.cos(col), jnp.cos(col)], axis=1)
    sin = jnp.concatenate([-jnp.sin(row), jnp.sin(row), -jnp.sin(col), jnp.sin(col)], axis=1)
    reps = LANES // HEAD_DIM
    return jnp.tile(cos, (1, reps)), jnp.tile(sin, (1, reps))


def _head_perm():
    cols = []
    for hq in HEAD_ORDER:
        cols.extend(range(hq * HEAD_DIM, (hq + 1) * HEAD_DIM))
    return jnp.array(cols, dtype=jnp.int32)


def _moe_plan(routed, counts, n_tiles_max):
    te_rows = EXPERT_TILE
    cnt = counts[0, :N_EXPERTS].astype(jnp.int32)
    padded = ((cnt + te_rows - 1) // te_rows) * te_rows
    ends = jnp.cumsum(padded)
    starts = ends - padded
    idx = routed[:, 0:2].astype(jnp.int32)
    rank = routed[:, 2:4].astype(jnp.int32)
    onehot = idx[:, :, None] == jnp.arange(N_EXPERTS, dtype=jnp.int32)[None, None, :]
    pos = rank + jnp.sum(jnp.where(onehot, starts[None, None, :], 0), axis=-1)
    tile_start = jnp.arange(n_tiles_max, dtype=jnp.int32) * te_rows
    tile_expert = jnp.minimum(
        jnp.sum((tile_start[:, None] >= ends[None, :]).astype(jnp.int32), axis=1), N_EXPERTS - 1)
    n_valid = (ends[-1] // te_rows).astype(jnp.int32).reshape(1)
    pad_start = (starts + cnt).astype(jnp.int32)
    return pos.reshape(-1), tile_expert.astype(jnp.int32), n_valid, pad_start


def kernel(x, c, ctx, c_ctx, w_mod, b_mod, norm_g, w_in, sinks, w_att_o, pool_w, pool_scale, w_pool_o,
           w_out, w_ffn_gate, w_ffn_up, w_ffn_down, w_router, b_router, w_exp_gate, w_exp_up,
           w_exp_down):
    n_batch, seq, d = x.shape
    n_ctx = ctx.shape[1]
    depth = w_mod.shape[0]
    tm = TOKEN_TILE
    n_lat = n_batch * seq
    t_all = n_lat + n_batch * n_ctx
    assert seq % tm == 0 and (n_batch * n_ctx) % tm == 0 and n_lat % n_ctx == 0
    assert (n_ctx % tm == 0) or (tm % n_ctx == 0)
    n_lat_tiles = n_lat // tm
    tiles_per_batch = seq // tm
    geom = dict(n_lat_tiles=n_lat_tiles, tiles_per_batch=tiles_per_batch, n_batch=n_batch)

    c8 = jnp.zeros((8, d), f32).at[:n_batch].set(c).at[n_batch].set(c_ctx)
    mod = _modulation(c8, w_mod, b_mod).reshape(depth, 8, N_MOD, d)
    cos, sin = _rope_tables(seq)
    perm = _head_perm()

    v0 = ATT_WIDTH + KV_WIDTH
    qkp_cols = jnp.concatenate([perm, jnp.arange(ATT_WIDTH, v0, dtype=jnp.int32),
                                jnp.arange(v0 + KV_WIDTH, QKVP_WIDTH, dtype=jnp.int32)])
    w_qkp = w_in[:, :, qkp_cols].astype(bf16)
    w_vt = jnp.swapaxes(w_in[:, :, v0:v0 + KV_WIDTH], 1, 2).astype(bf16)
    w_gate = (0.5 * w_in[:, :, QKVP_WIDTH:]).astype(bf16)
    w_att = w_att_o[:, perm, :].astype(bf16)
    w_pool = w_pool_o.astype(bf16)
    w_o = (0.5 * w_out).astype(bf16)
    poolw = pool_w.astype(bf16)
    sinks_p = sinks[:, jnp.array(HEAD_ORDER)]
    w_r = jnp.zeros((w_router.shape[0], d, LANES), bf16).at[:, :, :N_EXPERTS].set(w_router.astype(bf16))
    b_r = jnp.zeros((b_router.shape[0], 1, LANES), f32).at[:, 0, :N_EXPERTS].set(b_router)
    wd_g, wd_u, wd_d = (w.astype(bf16) for w in (w_ffn_gate, w_ffn_up, w_ffn_down))
    we_g, we_u, we_d = (w.astype(bf16) for w in (w_exp_gate, w_exp_up, w_exp_down))

    xa = jnp.concatenate([x.reshape(n_lat, d), ctx.reshape(n_batch * n_ctx, d)], axis=0)

    n_e = w_exp_gate.shape[1]
    we_g, we_u, we_d = (w.reshape((-1,) + w.shape[2:]) for w in (we_g, we_u, we_d))
    pscale = pool_scale[:, None, :]
    ns = d // LANES

    for l in range(depth):
        last = l == depth - 1
        moe = l % 2 == 1
        n_rows = n_lat if last else t_all
        mod_l = mod[l]
        g_l = norm_g[l:l + 1]

        q, k, vt, p = _inproj(xa, mod_l, g_l, cos, sin, l, w_qkp, w_vt, **geom)
        att = _attention(sinks_p[l], q, k, vt, n_batch=n_batch, seq=seq, n_ctx=n_ctx, with_ctx=not last)
        router = (w_r, b_r) if moe else None
        outs = _merge(xa, att, p, mod_l, g_l, l, w_gate, w_att, w_pool, w_o, poolw, pscale, router,
                      n_rows=n_rows, seq=seq, n_ctx=n_ctx, **geom)
        i = l // 2
        if not moe:
            x1, hf = outs
            n_tiles = n_rows // tm
            xa = _ffn(jnp.full((n_tiles,), i, jnp.int32), jnp.full((1,), n_tiles, jnp.int32), hf,
                      wd_g, wd_u, wd_d, (x1, mod_l, g_l, tiles_per_batch, n_batch),
                      tm=tm, n_chunks=1, tiled_rows=False)
        else:
            x1, hf, logits = outs
            routed, counts = _route(logits)
            n_tiles_max = (2 * n_rows) // EXPERT_TILE + N_EXPERTS
            pos, tile_expert, n_valid, pad_start = _moe_plan(routed, counts, n_tiles_max)
            xs = _dispatch(pad_start, pos, hf, (n_tiles_max + 1) * EXPERT_TILE, ns)
            y = _ffn(tile_expert + i * n_e, n_valid, xs, we_g, we_u, we_d, None,
                     tm=EXPERT_TILE, n_chunks=2, tiled_rows=True)
            xa = _combine(pos, y, routed, x1, mod_l, g_l, tiles_per_batch=tiles_per_batch,
                          n_batch=n_batch)
    return xa[:n_lat].reshape(n_batch, seq, d)
```
